```python
import math
import jax, jax.numpy as jnp
from jax import lax
import numpy as np

D_MODEL = 1024
BATCH = 16
SEQ = 2048
DEPTH = 1
DEC_BATCH = 32
DEC_SEQ = 4
PAST_LEN = 16384
PAGE_SIZE = 128

MIX_WIDTH = D_MODEL
CONV_CH = MIX_WIDTH // 2
CONV_K = 3
N_HEADS = 4
V_HEAD = (MIX_WIDTH - CONV_CH) // N_HEADS
QK_HEAD = V_HEAD // 2
ROT_DIM = QK_HEAD // 4
ROPE_THETA = 500000.0
D_FF = -(-8 * D_MODEL // (3 * 256)) * 256
Q_BLOCK = 128
N_MOD = 6
ALPHA = (2.0 * DEPTH) ** 0.25
BETA = (8.0 * DEPTH) ** -0.25
LN_EPS = 1e-5
SPLITS = (CONV_CH, CONV_CH, CONV_CH, N_HEADS * 2 * QK_HEAD, N_HEADS * 2 * QK_HEAD, N_HEADS * V_HEAD)
SPLIT_IDX = tuple(int(i) for i in np.cumsum(SPLITS)[:-1])
IN_WIDTH = sum(SPLITS)

kernel_name = 'hybrid_shortconv_diffattn_step'


def layer_norm(x, g, b):
    xf = x.astype(jnp.float32)
    mu = jnp.mean(xf, -1, keepdims=True)
    var = jnp.mean(jnp.square(xf - mu), -1, keepdims=True)
    return ((xf - mu) * lax.rsqrt(var + LN_EPS) * g.astype(jnp.float32) + b.astype(jnp.float32)).astype(x.dtype)


def rope_tables(pos):
    inv = ROPE_THETA ** (-jnp.arange(0, ROT_DIM, 2, dtype=jnp.float32) / ROT_DIM)
    ang = pos.astype(jnp.float32)[:, None] * inv[None, :]
    return jnp.cos(ang), jnp.sin(ang)


def apply_rope(x, cos, sin):
    c = cos[None, :, None, None, :]
    s = sin[None, :, None, None, :]
    xr = x[..., :ROT_DIM].astype(jnp.float32)
    x1, x2 = xr[..., :ROT_DIM // 2], xr[..., ROT_DIM // 2:]
    rot = jnp.concatenate([x1 * c - x2 * s, x2 * c + x1 * s], axis=-1)
    return jnp.concatenate([rot.astype(x.dtype), x[..., ROT_DIM:]], axis=-1)


def adaln(c, w, b):
    m = jax.nn.silu(c) @ w + b
    return m.reshape(c.shape[0], N_MOD, D_MODEL)


def modulate(x, shift, scale):
    return x * (1 + scale[:, None, :]) + shift[:, None, :]


def mixer_inputs(x, mod, w_in, cos, sin):
    b, t, _ = x.shape
    z = modulate(x, mod[:, 0], mod[:, 1]) @ w_in
    h, gb, gc, q, k, v = jnp.split(z, SPLIT_IDX, axis=-1)
    q = apply_rope(q.reshape(b, t, N_HEADS, 2, QK_HEAD), cos, sin)
    k = apply_rope(k.reshape(b, t, N_HEADS, 2, QK_HEAD), cos, sin)
    v = v.reshape(b, t, N_HEADS, V_HEAD)
    return h, gb, gc, q, k, v


def short_conv(u_ext, w):
    t = u_ext.shape[1] - (CONV_K - 1)
    return sum(w[j] * u_ext[:, j:j + t] for j in range(CONV_K))


def diff_attend(q, q_pos, k, v, k_pos, lam):
    s = jnp.einsum('bqhcd,bkhcd->bhcqk', q, k).astype(jnp.float32) * (QK_HEAD ** -0.5)
    mask = k_pos[None, :] <= q_pos[:, None]
    s = jnp.where(mask, s, -jnp.inf)
    p = jax.nn.softmax(s, axis=-1)
    a = p[:, :, 0] - lam * p[:, :, 1]
    return jnp.einsum('bhqk,bkhd->bqhd', a, v.astype(jnp.float32))


def head_out(o, g, lam_init, dtype):
    o = o * lax.rsqrt(jnp.mean(jnp.square(o), -1, keepdims=True) + LN_EPS) * g.astype(jnp.float32)
    o = o * (1.0 - lam_init)
    b, t = o.shape[:2]
    return o.reshape(b, t, N_HEADS * V_HEAD).astype(dtype)


def finish(x, y_mix, mod, w_out, ln1_g, ln1_b, w_gate, w_up, w_down, ln2_g, ln2_b):
    x = layer_norm(ALPHA * x + mod[:, 2][:, None, :] * (y_mix @ w_out), ln1_g, ln1_b)
    hf = modulate(x, mod[:, 3], mod[:, 4])
    f = (jax.nn.silu(hf @ w_gate) * (hf @ w_up)) @ w_down
    return layer_norm(ALPHA * x + mod[:, 5][:, None, :] * f, ln2_g, ln2_b)


def setup_inputs(seed: int = 0) -> dict:
    key = jax.random.key(seed)
    ks = jax.random.split(key, 26)
    f32 = jnp.float32
    n_pages = PAST_LEN // PAGE_SIZE
    n_used = DEC_BATCH * n_pages
    n_pool = (5 * n_used + 3) // 4

    def nrm(k, shape, s):
        return jax.random.normal(k, shape, f32) * s

    sd = D_MODEL ** -0.5
    v_col = jnp.concatenate([jnp.ones((IN_WIDTH - N_HEADS * V_HEAD,), f32),
                             jnp.full((N_HEADS * V_HEAD,), BETA, f32)])
    page_table = jax.random.permutation(ks[7], n_pool)[:n_used].reshape(DEC_BATCH, n_pages).astype(jnp.int32)
    return dict(
        x_prompt=nrm(ks[0], (BATCH, SEQ, D_MODEL), 1.0),
        x_sample=nrm(ks[1], (DEC_BATCH, DEC_SEQ, D_MODEL), 1.0),
        c_prompt=nrm(ks[2], (BATCH, D_MODEL), 1.0),
        c_sample=nrm(ks[3], (DEC_BATCH, D_MODEL), 1.0),
        cache_k=nrm(ks[4], (DEPTH, n_pool, PAGE_SIZE, N_HEADS, 2 * QK_HEAD), 1.0),
        cache_v=nrm(ks[5], (DEPTH, n_pool, PAGE_SIZE, N_HEADS, V_HEAD), 1.0),
        state_conv=nrm(ks[6], (DEPTH, DEC_BATCH, CONV_K - 1, CONV_CH), 1.0),
        page_table=page_table,
        w_ada=nrm(ks[8], (DEPTH, D_MODEL, N_MOD * D_MODEL), 0.5 * sd),
        b_ada=nrm(ks[9], (DEPTH, N_MOD * D_MODEL), 0.01),
        w_in=nrm(ks[10], (DEPTH, D_MODEL, IN_WIDTH), sd) * v_col,
        conv_w=nrm(ks[11], (DEPTH, CONV_K, CONV_CH), CONV_K ** -0.5),
        lambda_q1=nrm(ks[12], (DEPTH, QK_HEAD), 0.1),
        lambda_k1=nrm(ks[13], (DEPTH, QK_HEAD), 0.1),
        lambda_q2=nrm(ks[14], (DEPTH, QK_HEAD), 0.1),
        lambda_k2=nrm(ks[15], (DEPTH, QK_HEAD), 0.1),
        subln_g=1.0 + nrm(ks[16], (DEPTH, V_HEAD), 0.02),
        w_out=nrm(ks[17], (DEPTH, MIX_WIDTH, D_MODEL), BETA * MIX_WIDTH ** -0.5),
        ln1_g=1.0 + nrm(ks[18], (DEPTH, D_MODEL), 0.02),
        ln1_b=nrm(ks[19], (DEPTH, D_MODEL), 0.02),
        w_gate=nrm(ks[20], (DEPTH, D_MODEL, D_FF), sd),
        w_up=nrm(ks[21], (DEPTH, D_MODEL, D_FF), sd),
        w_down=nrm(ks[22], (DEPTH, D_FF, D_MODEL), BETA * D_FF ** -0.5),
        ln2_g=1.0 + nrm(ks[23], (DEPTH, D_MODEL), 0.02),
        ln2_b=nrm(ks[24], (DEPTH, D_MODEL), 0.02),
    )


def reference(x_prompt, x_sample, c_prompt, c_sample, cache_k, cache_v, state_conv, page_table,
              w_ada, b_ada, w_in, conv_w, lambda_q1, lambda_k1, lambda_q2, lambda_k2, subln_g,
              w_out, ln1_g, ln1_b, w_gate, w_up, w_down, ln2_g, ln2_b):
    bp, s_len = x_prompt.shape[0], x_prompt.shape[1]
    bs, t_len = x_sample.shape[0], x_sample.shape[1]
    past = page_table.shape[1] * cache_k.shape[2]
    n_blk = s_len // Q_BLOCK

    pos_p = jnp.arange(s_len)
    cos_p, sin_p = rope_tables(pos_p)
    q_pos_s = past + jnp.arange(t_len)
    cos_s, sin_s = rope_tables(q_pos_s)
    k_pos_s = jnp.arange(past + t_len)
    q_pos_blocks = pos_p.reshape(n_blk, Q_BLOCK)

    xp, xs = x_prompt, x_sample
    kp_l, vp_l, cp_l, ks_l, vs_l, cs_l = [], [], [], [], [], []
    for l in range(DEPTH):
        lam_init = 0.8 - 0.6 * math.exp(-0.3 * l)
        lam = (jnp.exp(jnp.sum(lambda_q1[l].astype(jnp.float32) * lambda_k1[l].astype(jnp.float32)))
               - jnp.exp(jnp.sum(lambda_q2[l].astype(jnp.float32) * lambda_k2[l].astype(jnp.float32)))
               + lam_init)

        mod_p = adaln(c_prompt, w_ada[l], b_ada[l])
        h, gb, gc, q, k, v = mixer_inputs(xp, mod_p, w_in[l], cos_p, sin_p)
        u_ext = jnp.concatenate([jnp.zeros((bp, CONV_K - 1, CONV_CH), h.dtype), gc * h], axis=1)
        y_conv = gb * short_conv(u_ext, conv_w[l])
        qb = jnp.moveaxis(q.reshape(bp, n_blk, Q_BLOCK, N_HEADS, 2, QK_HEAD), 1, 0)
        ob = lax.map(lambda a: diff_attend(a[0], a[1], k, v, pos_p, lam), (qb, q_pos_blocks))
        o = jnp.moveaxis(ob, 0, 1).reshape(bp, s_len, N_HEADS, V_HEAD)
        y_attn = head_out(o, subln_g[l], lam_init, xp.dtype)
        xp = finish(xp, jnp.concatenate([y_conv, y_attn], axis=-1), mod_p, w_out[l],
                    ln1_g[l], ln1_b[l], w_gate[l], w_up[l], w_down[l], ln2_g[l], ln2_b[l])
        kp_l.append(k.reshape(bp, s_len, N_HEADS, 2 * QK_HEAD))
        vp_l.append(v)
        cp_l.append(u_ext[:, -(CONV_K - 1):])

        mod_s = adaln(c_sample, w_ada[l], b_ada[l])
        h, gb, gc, q, k, v = mixer_inputs(xs, mod_s, w_in[l], cos_s, sin_s)
        u_ext = jnp.concatenate([state_conv[l].astype(h.dtype), gc * h], axis=1)
        y_conv = gb * short_conv(u_ext, conv_w[l])
        k_past = cache_k[l, page_table].reshape(bs, past, N_HEADS, 2, QK_HEAD)
        v_past = cache_v[l, page_table].reshape(bs, past, N_HEADS, V_HEAD)
        k_all = jnp.concatenate([k_past.astype(k.dtype), k], axis=1)
        v_all = jnp.concatenate([v_past.astype(v.dtype), v], axis=1)
        o = diff_attend(q, q_pos_s, k_all, v_all, k_pos_s, lam)
        y_attn = head_out(o, subln_g[l], lam_init, xs.dtype)
        xs = finish(xs, jnp.concatenate([y_conv, y_attn], axis=-1), mod_s, w_out[l],
                    ln1_g[l], ln1_b[l], w_gate[l], w_up[l], w_down[l], ln2_g[l], ln2_b[l])
        ks_l.append(k.reshape(bs, t_len, N_HEADS, 2 * QK_HEAD))
        vs_l.append(v)
        cs_l.append(u_ext[:, -(CONV_K - 1):])

    return (xp, xs, jnp.stack(kp_l), jnp.stack(vp_l), jnp.stack(cp_l),
            jnp.stack(ks_l), jnp.stack(vs_l), jnp.stack(cs_l))
```

```python
import functools
import math

import jax
import jax.numpy as jnp
from jax import lax
from jax.experimental import pallas as pl
from jax.experimental.pallas import tpu as pltpu

F32 = jnp.float32
BF16 = jnp.bfloat16

N_HEADS = 4
QK_HEAD = 64
V_HEAD = 128
HEAD_W = 2 * QK_HEAD
CONV_K = 3
ROT_DIM = QK_HEAD // 4
ROPE_THETA = 500000.0
N_MOD = 6
LN_EPS = 1e-5
LANES = 128
SUBLANES = 8
VMEM_LIMIT_BYTES = 56 * 1024 * 1024
NEG_INF = float("-inf")


def _params(n_axes):
    return pltpu.CompilerParams(dimension_semantics=("arbitrary",) * n_axes,
                                vmem_limit_bytes=VMEM_LIMIT_BYTES)


def _resident(shape):
    return pl.BlockSpec(shape, lambda *_: (0,) * len(shape), pipeline_mode=pl.Buffered(1))


def _largest_divisor(n, cap):
    return max(d for d in range(1, min(n, cap) + 1) if n % d == 0)


def _layer_norm(x, g, b):
    mu = jnp.mean(x, axis=-1, keepdims=True)
    xc = x - mu
    var = jnp.mean(xc * xc, axis=-1, keepdims=True)
    return xc * lax.rsqrt(var + LN_EPS) * g + b


def _adaln_kernel(c_ref, w_ref, b_ref, o_ref):
    a = jax.nn.silu(c_ref[...]).astype(BF16)
    o_ref[...] = jnp.dot(a, w_ref[...].astype(BF16), preferred_element_type=F32) + b_ref[...]


def _adaln(c, w, b):
    n, d = c.shape
    n_out = w.shape[1]
    tn = 768 if n_out % 768 == 0 else n_out
    return pl.pallas_call(
        _adaln_kernel,
        grid=(n_out // tn,),
        in_specs=[pl.BlockSpec((n, d), lambda j: (0, 0)),
                  pl.BlockSpec((d, tn), lambda j: (0, j)),
                  pl.BlockSpec((1, tn), lambda j: (0, j))],
        out_specs=pl.BlockSpec((n, tn), lambda j: (0, j)),
        out_shape=jax.ShapeDtypeStruct((n, n_out), F32),
        compiler_params=_params(1),
        name="adaln",
    )(c, w, b.reshape(1, n_out))


def _rope_tables(pos):
    inv = ROPE_THETA ** (-jnp.arange(0, ROT_DIM, 2, dtype=F32) / ROT_DIM)
    ang = pos.astype(F32)[:, None] * inv[None, :]
    cos, sin = jnp.cos(ang), jnp.sin(ang)
    t, half = pos.shape[0], ROT_DIM // 2
    ones = jnp.ones((t, QK_HEAD - ROT_DIM), F32)
    c = jnp.concatenate([cos, cos, ones], axis=1)
    s_up = jnp.concatenate([-sin, jnp.zeros((t, QK_HEAD - half), F32)], axis=1)
    s_dn = jnp.concatenate([jnp.zeros((t, half), F32), sin, jnp.zeros((t, QK_HEAD - ROT_DIM), F32)], axis=1)
    return jnp.stack([jnp.tile(a, (1, LANES // QK_HEAD)) for a in (c, s_up, s_dn)])


def _rope(x, tab_ref):
    c, s_up, s_dn = tab_ref[0], tab_ref[1], tab_ref[2]
    half = ROT_DIM // 2
    cols = []
    for i in range(x.shape[1] // LANES):
        xc = x[:, i * LANES:(i + 1) * LANES]
        cols.append(xc * c + pltpu.roll(xc, LANES - half, 1) * s_up + pltpu.roll(xc, half, 1) * s_dn)
    return jnp.concatenate(cols, axis=1)


def _prompt_in_kernel(x_ref, mod_ref, w_ref, cw_ref, tab_ref,
                      yconv_ref, q_ref, k_ref, v_ref, cst_ref, carry_ref, ubuf_ref, *, ts, cc):
    s, b = pl.program_id(0), pl.program_id(1)
    mod = mod_ref[0]
    xm = (x_ref[0] * (1.0 + mod[1:2]) + mod[0:1]).astype(BF16)

    z = jnp.dot(xm, w_ref[:, 0:3 * cc], preferred_element_type=F32)
    h, gb, gc = z[:, 0:cc], z[:, cc:2 * cc], z[:, 2 * cc:3 * cc]
    u = gc * h

    @pl.when(s == 0)
    def _():
        ubuf_ref[0:SUBLANES] = jnp.zeros((SUBLANES, cc), F32)

    @pl.when(s > 0)
    def _():
        ubuf_ref[0:SUBLANES] = carry_ref[b]

    ubuf_ref[SUBLANES:SUBLANES + ts] = u
    u1 = ubuf_ref[SUBLANES - 1:SUBLANES - 1 + ts]
    u2 = ubuf_ref[SUBLANES - 2:SUBLANES - 2 + ts]
    cw = cw_ref[...]
    yconv_ref[0] = (gb * (cw[0:1] * u2 + cw[1:2] * u1 + cw[2:3] * u)).astype(BF16)
    carry_ref[b] = u[ts - SUBLANES:]
    cst_ref[0, 0] = u[ts - (CONV_K - 1):]

    z = jnp.dot(xm, w_ref[:, 3 * cc:], preferred_element_type=F32)
    hw = N_HEADS * HEAD_W
    q_ref[0] = (_rope(z[:, 0:hw], tab_ref) * (QK_HEAD ** -0.5)).astype(BF16)
    k_ref[0] = _rope(z[:, hw:2 * hw], tab_ref)
    v_ref[0] = z[:, 2 * hw:]


def _prompt_in(x, mod, w_in, conv_w, tab, ts):
    bsz, seq, d = x.shape
    cc = conv_w.shape[1]
    hw = N_HEADS * HEAD_W
    n_in = w_in.shape[1]
    tok = lambda s, b: (b, s, 0)
    kern = functools.partial(_prompt_in_kernel, ts=ts, cc=cc)
    return pl.pallas_call(
        kern,
        grid=(seq // ts, bsz),
        in_specs=[pl.BlockSpec((1, ts, d), tok),
                  pl.BlockSpec((1, N_MOD, d), lambda s, b: (b, 0, 0)),
                  _resident((d, n_in)),
                  _resident((CONV_K, cc)),
                  pl.BlockSpec((3, ts, LANES), lambda s, b: (0, s, 0))],
        out_specs=[pl.BlockSpec((1, ts, cc), tok),
                   pl.BlockSpec((1, ts, hw), tok),
                   pl.BlockSpec((1, ts, hw), tok),
                   pl.BlockSpec((1, ts, N_HEADS * V_HEAD), tok),
                   pl.BlockSpec((1, 1, CONV_K - 1, cc), lambda s, b: (s, b, 0, 0))],
        out_shape=[jax.ShapeDtypeStruct((bsz, seq, cc), BF16),
                   jax.ShapeDtypeStruct((bsz, seq, hw), BF16),
                   jax.ShapeDtypeStruct((bsz, seq, hw), F32),
                   jax.ShapeDtypeStruct((bsz, seq, N_HEADS * V_HEAD), F32),
                   jax.ShapeDtypeStruct((seq // ts, bsz, CONV_K - 1, cc), F32)],
        scratch_shapes=[pltpu.VMEM((bsz, SUBLANES, cc), F32),
                        pltpu.VMEM((ts + SUBLANES, cc), F32)],
        compiler_params=_params(2),
        name="prompt_in",
    )(x, mod, w_in, conv_w, tab)


def _lambda_value(lam_ref, lam_init):
    lp = lam_ref[...]
    a = jnp.sum(lp[0:1] * lp[1:2], axis=1, keepdims=True)
    b = jnp.sum(lp[2:3] * lp[3:4], axis=1, keepdims=True)
    return jnp.exp(a) - jnp.exp(b) + lam_init


def _head_out(o1, o2, lam, g, lam_init):
    d = o1 - lam * o2
    d = d * lax.rsqrt(jnp.mean(d * d, axis=-1, keepdims=True) + LN_EPS) * g
    return d * (1.0 - lam_init)


def _softmax_step(carry, s, v):
    m, l, acc = carry
    m_new = jnp.maximum(m, jnp.max(s, axis=1, keepdims=True))
    alpha = jnp.exp(m - m_new)
    p = jnp.exp(s - m_new)
    l = alpha * l + jnp.sum(p, axis=1, keepdims=True)
    acc = alpha * acc + jnp.dot(p.astype(BF16), v, preferred_element_type=F32)
    return m_new, l, acc


def _prompt_attn_kernel(q_ref, k_ref, v_ref, lam_ref, g_ref, o_ref, *, seq, tq, tk, lam_init):
    lam = _lambda_value(lam_ref, lam_init)
    lane = lax.broadcasted_iota(jnp.int32, (tq, HEAD_W), 1)
    nt = (((1,), (1,)), ((), ()))

    for qi in range(seq // tq):
        q = q_ref[0, qi * tq:(qi + 1) * tq, :]
        zero = jnp.zeros_like(q)
        qq = jnp.concatenate([jnp.where(lane < QK_HEAD, q, zero), jnp.where(lane >= QK_HEAD, q, zero)], axis=0)

        def tile(k0, width, carry, masked):
            k = k_ref[0, pl.ds(k0, width), :].astype(BF16)
            v = v_ref[0, pl.ds(k0, width), :].astype(BF16)
            s = lax.dot_general(qq, k, nt, preferred_element_type=F32)
            if masked:
                row = lax.broadcasted_iota(jnp.int32, (2 * tq, width), 0)
                col = lax.broadcasted_iota(jnp.int32, (2 * tq, width), 1)
                q_pos = jnp.where(row >= tq, row - tq, row)
                s = jnp.where(col <= q_pos, s, NEG_INF)
            return _softmax_step(carry, s, v)

        carry = (jnp.full((2 * tq, 1), NEG_INF, F32), jnp.zeros((2 * tq, 1), F32), jnp.zeros((2 * tq, V_HEAD), F32))
        n_full = (qi * tq) // tk
        if n_full:
            carry = lax.fori_loop(
                0, n_full, lambda i, c: tile(pl.multiple_of(i * tk, tk), tk, c, False), carry)
        carry = tile(qi * tq, tq, carry, True)
        _, l, acc = carry
        o = acc / l
        y = _head_out(o[:tq], o[tq:], lam, g_ref[...], lam_init)
        o_ref[0, qi * tq:(qi + 1) * tq, :] = y.astype(BF16)


def _prompt_attn(q, k, v, lam_params, g, lam_init, tq, tk):
    bsz, seq, _ = q.shape
    blk = pl.BlockSpec((1, seq, HEAD_W), lambda b, h: (b, 0, h))
    kern = functools.partial(_prompt_attn_kernel, seq=seq, tq=tq, tk=tk, lam_init=lam_init)
    return pl.pallas_call(
        kern,
        grid=(bsz, N_HEADS),
        in_specs=[blk, blk, blk,
                  pl.BlockSpec((4, QK_HEAD), lambda b, h: (0, 0)),
                  pl.BlockSpec((1, V_HEAD), lambda b, h: (0, 0))],
        out_specs=blk,
        out_shape=jax.ShapeDtypeStruct((bsz, seq, N_HEADS * V_HEAD), BF16),
        compiler_params=_params(2),
        name="prompt_attn",
    )(q, k, v, lam_params, g)


def _finish_math(x, yc, ya, gate1, shift2, scale2, gate2, wo_ref, ln1g, ln1b, wg_ref, wu_ref, wd_ref,
                 ln2g, ln2b, alpha):
    cc = yc.shape[1]
    y = (jnp.dot(yc, wo_ref[0:cc], preferred_element_type=F32)
         + jnp.dot(ya, wo_ref[cc:], preferred_element_type=F32))
    x1 = _layer_norm(alpha * x + gate1 * y, ln1g, ln1b)
    hf = (x1 * (1.0 + scale2) + shift2).astype(BF16)
    g = jnp.dot(hf, wg_ref[...], preferred_element_type=F32)
    u = jnp.dot(hf, wu_ref[...], preferred_element_type=F32)
    act = (jax.nn.silu(g) * u).astype(BF16)
    f = jnp.dot(act, wd_ref[...], preferred_element_type=F32)
    return _layer_norm(alpha * x1 + gate2 * f, ln2g, ln2b)


def _prompt_finish_kernel(x_ref, yc_ref, ya_ref, mod_ref, wo_ref, ln1_ref, wg_ref, wu_ref, wd_ref, ln2_ref,
                          o_ref, *, alpha):
    mod = mod_ref[0]
    o_ref[0] = _finish_math(x_ref[0], yc_ref[0], ya_ref[0], mod[2:3], mod[3:4], mod[4:5], mod[5:6],
                            wo_ref, ln1_ref[0:1], ln1_ref[1:2], wg_ref, wu_ref, wd_ref,
                            ln2_ref[0:1], ln2_ref[1:2], alpha)


def _sample_finish_kernel(x_ref, yc_ref, o1_ref, o2_ref, lam_ref, g_ref, mod_ref, wo_ref, ln1_ref,
                          wg_ref, wu_ref, wd_ref, ln2_ref, o_ref, *, alpha, lam_init):
    lam = _lambda_value(lam_ref, lam_init)
    o1, o2 = o1_ref[...], o2_ref[...]
    ya = jnp.concatenate(
        [_head_out(o1[:, h * V_HEAD:(h + 1) * V_HEAD], o2[:, h * V_HEAD:(h + 1) * V_HEAD], lam, g_ref[...], lam_init)
         for h in range(N_HEADS)], axis=1).astype(BF16)
    o_ref[...] = _finish_math(x_ref[...], yc_ref[...], ya, mod_ref[2], mod_ref[3], mod_ref[4], mod_ref[5],
                              wo_ref, ln1_ref[0:1], ln1_ref[1:2], wg_ref, wu_ref, wd_ref,
                              ln2_ref[0:1], ln2_ref[1:2], alpha)


def _prompt_finish(x, yc, ya, mod, w_out, ln1, w_gate, w_up, w_down, ln2, alpha, tm):
    bsz, seq, d = x.shape
    cc, d_ff = yc.shape[2], w_gate.shape[1]
    tok = lambda b, s: (b, s, 0)
    kern = functools.partial(_prompt_finish_kernel, alpha=alpha)
    return pl.pallas_call(
        kern,
        grid=(bsz, seq // tm),
        in_specs=[pl.BlockSpec((1, tm, d), tok),
                  pl.BlockSpec((1, tm, cc), tok),
                  pl.BlockSpec((1, tm, d - cc), tok),
                  pl.BlockSpec((1, N_MOD, d), lambda b, s: (b, 0, 0)),
                  _resident((d, d)), _resident((2, d)),
                  _resident((d, d_ff)), _resident((d, d_ff)), _resident((d_ff, d)), _resident((2, d))],
        out_specs=pl.BlockSpec((1, tm, d), tok),
        out_shape=jax.ShapeDtypeStruct((bsz, seq, d), F32),
        compiler_params=_params(2),
        name="prompt_finish",
    )(x, yc, ya, mod, w_out, ln1, w_gate, w_up, w_down, ln2)


def _sample_finish(x, yc, o1, o2, lam_params, g, mod_rows, w_out, ln1, w_gate, w_up, w_down, ln2, alpha, lam_init):
    n, d = x.shape
    kern = functools.partial(_sample_finish_kernel, alpha=alpha, lam_init=lam_init)
    args = (x, yc, o1, o2, lam_params, g, mod_rows, w_out, ln1, w_gate, w_up, w_down, ln2)
    return pl.pallas_call(
        kern,
        grid=(1,),
        in_specs=[_resident(a.shape) for a in args],
        out_specs=pl.BlockSpec((n, d), lambda i: (0, 0)),
        out_shape=jax.ShapeDtypeStruct((n, d), F32),
        compiler_params=_params(1),
        name="sample_finish",
    )(*args)


def _sample_in_kernel(x_ref, mod_ref, w_ref, cw_ref, st1_ref, st2_ref, tab_ref,
                      yconv_ref, q_ref, k_ref, v_ref, u_ref, *, t_len, cc):
    xm = (x_ref[...] * (1.0 + mod_ref[1]) + mod_ref[0]).astype(BF16)
    z = jnp.dot(xm, w_ref[...], preferred_element_type=F32)
    h, gb, gc = z[:, 0:cc], z[:, cc:2 * cc], z[:, 2 * cc:3 * cc]
    u = gc * h
    n = u.shape[0]
    t = lax.broadcasted_iota(jnp.int32, (n, cc), 0) % t_len
    u1 = jnp.where(t >= 1, pltpu.roll(u, 1, 0), st1_ref[...])
    u2 = jnp.where(t >= 2, pltpu.roll(u, 2, 0), st2_ref[...])
    cw = cw_ref[...]
    yconv_ref[...] = (gb * (cw[0:1] * u2 + cw[1:2] * u1 + cw[2:3] * u)).astype(BF16)
    u_ref[...] = u
    hw = N_HEADS * HEAD_W
    q_ref[...] = (_rope(z[:, 3 * cc:3 * cc + hw], tab_ref) * (QK_HEAD ** -0.5)).astype(BF16)
    k_ref[...] = _rope(z[:, 3 * cc + hw:3 * cc + 2 * hw], tab_ref)
    v_ref[...] = z[:, 3 * cc + 2 * hw:]


def _sample_in(x, mod_rows, w_in, conv_w, st1, st2, tab, t_len):
    n, _ = x.shape
    cc = conv_w.shape[1]
    hw = N_HEADS * HEAD_W
    kern = functools.partial(_sample_in_kernel, t_len=t_len, cc=cc)
    args = (x, mod_rows, w_in, conv_w, st1, st2, tab)
    out_shapes = [jax.ShapeDtypeStruct((n, cc), BF16), jax.ShapeDtypeStruct((n, hw), BF16),
                  jax.ShapeDtypeStruct((n, hw), F32), jax.ShapeDtypeStruct((n, N_HEADS * V_HEAD), F32),
                  jax.ShapeDtypeStruct((n, cc), F32)]
    return pl.pallas_call(
        kern,
        grid=(1,),
        in_specs=[_resident(a.shape) for a in args],
        out_specs=[pl.BlockSpec(o.shape, lambda i: (0, 0)) for o in out_shapes],
        out_shape=out_shapes,
        compiler_params=_params(1),
        name="sample_in",
    )(*args)


def _sample_attn_kernel(pt_ref, wq_ref, kn_ref, vn_ref, ck_ref, cv_ref, o_ref, kbuf, vbuf, sem,
                        *, n_pages, pages_per_chunk, page_base, t_len):
    b = pl.program_id(0)
    n_chunks = n_pages // pages_per_chunk
    page = kbuf.shape[2]
    width = kbuf.shape[3]
    nt = (((1,), (1,)), ((), ()))

    def copies(chunk, slot):
        out = []
        for g in range(pages_per_chunk):
            pg = pt_ref[b, chunk * pages_per_chunk + g] + page_base
            out.append(pltpu.make_async_copy(ck_ref.at[pg], kbuf.at[slot, g], sem.at[0, slot]))
            out.append(pltpu.make_async_copy(cv_ref.at[pg], vbuf.at[slot, g], sem.at[1, slot]))
        return out

    for c in copies(0, 0):
        c.start()

    wq = wq_ref[0]
    rows = wq.shape[0]

    def chunk_step(c, carry):
        slot = c % 2

        @pl.when(c + 1 < n_chunks)
        def _():
            for cp in copies(c + 1, 1 - slot):
                cp.start()

        for cp in copies(c, slot):
            cp.wait()
        k = kbuf[slot].reshape(pages_per_chunk * page, width).astype(BF16)
        v = vbuf[slot].reshape(pages_per_chunk * page, width).astype(BF16)
        s = lax.dot_general(wq, k, nt, preferred_element_type=F32)
        return _softmax_step(carry, s, v)

    carry = (jnp.full((rows, 1), NEG_INF, F32), jnp.zeros((rows, 1), F32), jnp.zeros((rows, width), F32))
    carry = lax.fori_loop(0, n_chunks, chunk_step, carry)

    pad = jnp.zeros((LANES - SUBLANES, width), BF16)
    kn = jnp.concatenate([kn_ref[0].astype(BF16), pad], axis=0)
    vn = jnp.concatenate([vn_ref[0].astype(BF16), pad], axis=0)
    s = lax.dot_general(wq, kn, nt, preferred_element_type=F32)
    step = lax.broadcasted_iota(jnp.int32, (rows, LANES), 0) % t_len
    key = lax.broadcasted_iota(jnp.int32, (rows, LANES), 1)
    s = jnp.where(key <= step, s, NEG_INF)
    _, l, acc = _softmax_step(carry, s, vn)
    o_ref[0] = acc / l


def _sample_attn(page_table, wq, k_new, v_new, cache_k, cache_v, page_base, t_len):
    bsz, n_pages = page_table.shape
    _, page, width = cache_k.shape
    rows = wq.shape[1]
    g = _largest_divisor(n_pages, 8)
    kern = functools.partial(_sample_attn_kernel, n_pages=n_pages, pages_per_chunk=g,
                             page_base=page_base, t_len=t_len)
    grid_spec = pltpu.PrefetchScalarGridSpec(
        num_scalar_prefetch=1,
        grid=(bsz,),
        in_specs=[pl.BlockSpec((1, rows, width), lambda b, pt: (b, 0, 0)),
                  pl.BlockSpec((1, SUBLANES, width), lambda b, pt: (b, 0, 0)),
                  pl.BlockSpec((1, SUBLANES, width), lambda b, pt: (b, 0, 0)),
                  pl.BlockSpec(memory_space=pl.ANY),
                  pl.BlockSpec(memory_space=pl.ANY)],
        out_specs=pl.BlockSpec((1, rows, width), lambda b, pt: (b, 0, 0)),
        scratch_shapes=[pltpu.VMEM((2, g, page, width), F32),
                        pltpu.VMEM((2, g, page, width), F32),
                        pltpu.SemaphoreType.DMA((2, 2))],
    )
    return pl.pallas_call(
        kern,
        grid_spec=grid_spec,
        out_shape=jax.ShapeDtypeStruct((bsz, rows, width), F32),
        compiler_params=_params(1),
        name="sample_attn",
    )(page_table, wq, k_new, v_new, cache_k, cache_v)


def _block_diag_queries(q, bsz, t_len):
    q5 = q.reshape(bsz, t_len, N_HEADS, 2, QK_HEAD).transpose(0, 2, 3, 1, 4)
    eye_h = jnp.eye(N_HEADS, dtype=q.dtype)
    eye_c = jnp.eye(2, dtype=q.dtype)
    w = q5[:, :, :, :, None, None, :] * eye_h[None, :, None, None, :, None, None] \
        * eye_c[None, None, :, None, None, :, None]
    return w.reshape(bsz, N_HEADS * 2 * t_len, N_HEADS * HEAD_W)


def _split_heads_output(o, bsz, t_len):
    o6 = o.reshape(bsz, N_HEADS, 2, t_len, N_HEADS, V_HEAD)
    own = jnp.stack([o6[:, h, :, :, h, :] for h in range(N_HEADS)], axis=1)
    own = own.transpose(2, 0, 3, 1, 4).reshape(2, bsz * t_len, N_HEADS * V_HEAD)
    return own[0], own[1]


def kernel(x_prompt, x_sample, c_prompt, c_sample, cache_k, cache_v, state_conv, page_table, w_ada, b_ada, w_in, conv_w, lambda_q1, lambda_k1, lambda_q2, lambda_k2, subln_g, w_out, ln1_g, ln1_b, w_gate, w_up, w_down, ln2_g, ln2_b):
    bp, s_len, d = x_prompt.shape
    bs, t_len, _ = x_sample.shape
    depth, n_pool, page = cache_k.shape[:3]
    past = page_table.shape[1] * page
    cc = conv_w.shape[2]
    alpha = (2.0 * depth) ** 0.25
    assert t_len <= SUBLANES and (bs * t_len) % SUBLANES == 0

    ts = min(512, s_len)
    tq = min(256, s_len)
    tm = min(512, s_len)

    tab_p = _rope_tables(jnp.arange(s_len))
    tab_s = jnp.tile(_rope_tables(past + jnp.arange(t_len)), (1, bs, 1))
    ck = cache_k.reshape(depth * n_pool, page, N_HEADS * HEAD_W)
    cv = cache_v.reshape(depth * n_pool, page, N_HEADS * V_HEAD)
    c_all = jnp.concatenate([c_prompt, c_sample], axis=0)

    xp, xs = x_prompt, x_sample.reshape(bs * t_len, d)
    kp_l, vp_l, cp_l, ks_l, vs_l, cs_l = [], [], [], [], [], []
    for l in range(depth):
        lam_init = 0.8 - 0.6 * math.exp(-0.3 * l)
        w_in_b, w_out_b = w_in[l].astype(BF16), w_out[l].astype(BF16)
        w_gate_b, w_up_b, w_down_b = w_gate[l].astype(BF16), w_up[l].astype(BF16), w_down[l].astype(BF16)
        lam_params = jnp.stack([lambda_q1[l], lambda_k1[l], lambda_q2[l], lambda_k2[l]])
        g = subln_g[l].reshape(1, V_HEAD)
        ln1 = jnp.stack([ln1_g[l], ln1_b[l]])
        ln2 = jnp.stack([ln2_g[l], ln2_b[l]])

        mod = _adaln(c_all, w_ada[l], b_ada[l]).reshape(bp + bs, N_MOD, d)
        mod_p = mod[:bp]
        mod_s = jnp.repeat(mod[bp:].transpose(1, 0, 2), t_len, axis=1)

        yconv, q, k, v, cst = _prompt_in(xp, mod_p, w_in_b, conv_w[l], tab_p, ts)
        yattn = _prompt_attn(q, k, v, lam_params, g, lam_init, tq, tq)
        xp = _prompt_finish(xp, yconv, yattn, mod_p, w_out_b, ln1, w_gate_b, w_up_b, w_down_b, ln2, alpha, tm)
        kp_l.append(k.reshape(bp, s_len, N_HEADS, HEAD_W))
        vp_l.append(v.reshape(bp, s_len, N_HEADS, V_HEAD))
        cp_l.append(cst[-1])

        st = state_conv[l]
        zero = jnp.zeros((bs, 1, cc), F32)
        st1 = jnp.concatenate([st[:, 1:2]] + [zero] * (t_len - 1), axis=1).reshape(bs * t_len, cc)
        st2 = jnp.concatenate([st[:, 0:1], st[:, 1:2]] + [zero] * (t_len - 2), axis=1).reshape(bs * t_len, cc)
        yconv_s, q_s, k_s, v_s, u_s = _sample_in(xs, mod_s, w_in_b, conv_w[l], st1, st2, tab_s, t_len)
        wq = _block_diag_queries(q_s, bs, t_len)
        pad_rows = lambda a: jnp.pad(a.reshape(bs, t_len, -1), ((0, 0), (0, SUBLANES - t_len), (0, 0)))
        o = _sample_attn(page_table, wq, pad_rows(k_s), pad_rows(v_s), ck, cv, l * n_pool, t_len)
        o1, o2 = _split_heads_output(o, bs, t_len)
        xs = _sample_finish(xs, yconv_s, o1, o2, lam_params, g, mod_s, w_out_b, ln1, w_gate_b, w_up_b, w_down_b,
                            ln2, alpha, lam_init)
        ks_l.append(k_s.reshape(bs, t_len, N_HEADS, HEAD_W))
        vs_l.append(v_s.reshape(bs, t_len, N_HEADS, V_HEAD))
        cs_l.append(u_s.reshape(bs, t_len, cc)[:, t_len - (CONV_K - 1):])

    return (xp, xs.reshape(bs, t_len, d), jnp.stack(kp_l), jnp.stack(vp_l), jnp.stack(cp_l),
            jnp.stack(ks_l), jnp.stack(vs_l), jnp.stack(cs_l))
```

```python
import functools
import math

import jax
import jax.numpy as jnp
from jax import lax
from jax.experimental import pallas as pl
from jax.experimental.pallas import tpu as pltpu

F32 = jnp.float32
BF16 = jnp.bfloat16

N_HEADS = 4
QK_HEAD = 64
V_HEAD = 128
HEAD_W = 2 * QK_HEAD
CONV_K = 3
ROT_DIM = QK_HEAD // 4
ROPE_THETA = 500000.0
N_MOD = 6
LN_EPS = 1e-5
LANES = 128
SUBLANES = 8
VMEM_LIMIT_BYTES = 56 * 1024 * 1024
NEG_INF = float("-inf")
Q_SCALE = QK_HEAD ** -0.5 * math.log2(math.e)


def _params(n_axes):
    return pltpu.CompilerParams(dimension_semantics=("arbitrary",) * n_axes,
                                vmem_limit_bytes=VMEM_LIMIT_BYTES)


def _resident(shape):
    return pl.BlockSpec(shape, lambda *_: (0,) * len(shape), pipeline_mode=pl.Buffered(1))


def _largest_divisor(n, cap):
    return max(d for d in range(1, min(n, cap) + 1) if n % d == 0)


def _layer_norm(x, g, b):
    mu = jnp.mean(x, axis=-1, keepdims=True)
    xc = x - mu
    var = jnp.mean(xc * xc, axis=-1, keepdims=True)
    return xc * lax.rsqrt(var + LN_EPS) * g + b


def _adaln_kernel(c_ref, w_ref, b_ref, o_ref):
    a = jax.nn.silu(c_ref[...]).astype(BF16)
    o_ref[...] = jnp.dot(a, w_ref[...].astype(BF16), preferred_element_type=F32) + b_ref[...]


def _adaln(c, w, b):
    n, d = c.shape
    n_out = w.shape[1]
    tn = 768 if n_out % 768 == 0 else n_out
    return pl.pallas_call(
        _adaln_kernel,
        grid=(n_out // tn,),
        in_specs=[pl.BlockSpec((n, d), lambda j: (0, 0)),
                  pl.BlockSpec((d, tn), lambda j: (0, j)),
                  pl.BlockSpec((1, tn), lambda j: (0, j))],
        out_specs=pl.BlockSpec((n, tn), lambda j: (0, j)),
        out_shape=jax.ShapeDtypeStruct((n, n_out), F32),
        compiler_params=_params(1),
        name="adaln",
    )(c, w, b.reshape(1, n_out))


def _rope_tables(pos):
    inv = ROPE_THETA ** (-jnp.arange(0, ROT_DIM, 2, dtype=F32) / ROT_DIM)
    ang = pos.astype(F32)[:, None] * inv[None, :]
    cos, sin = jnp.cos(ang), jnp.sin(ang)
    t, half = pos.shape[0], ROT_DIM // 2
    ones = jnp.ones((t, QK_HEAD - ROT_DIM), F32)
    c = jnp.concatenate([cos, cos, ones], axis=1)
    s_up = jnp.concatenate([-sin, jnp.zeros((t, QK_HEAD - half), F32)], axis=1)
    s_dn = jnp.concatenate([jnp.zeros((t, half), F32), sin, jnp.zeros((t, QK_HEAD - ROT_DIM), F32)], axis=1)
    return jnp.stack([jnp.tile(a, (1, LANES // QK_HEAD)) for a in (c, s_up, s_dn)])


def _rope(x, tab_ref):
    c, s_up, s_dn = tab_ref[0], tab_ref[1], tab_ref[2]
    half = ROT_DIM // 2
    cols = []
    for i in range(x.shape[1] // LANES):
        xc = x[:, i * LANES:(i + 1) * LANES]
        cols.append(xc * c + pltpu.roll(xc, LANES - half, 1) * s_up + pltpu.roll(xc, half, 1) * s_dn)
    return jnp.concatenate(cols, axis=1)


def _prompt_in_kernel(x_ref, mod_ref, w_ref, cw_ref, tab_ref,
                      yconv_ref, q_ref, k_ref, v_ref, cst_ref, carry_ref, ubuf_ref, *, ts, cc):
    s, b = pl.program_id(0), pl.program_id(1)
    mod = mod_ref[0]
    xm = (x_ref[0] * (1.0 + mod[1:2]) + mod[0:1]).astype(BF16)

    z = jnp.dot(xm, w_ref[:, 0:3 * cc], preferred_element_type=F32)
    h, gb, gc = z[:, 0:cc], z[:, cc:2 * cc], z[:, 2 * cc:3 * cc]
    u = gc * h

    @pl.when(s == 0)
    def _():
        ubuf_ref[0:SUBLANES] = jnp.zeros((SUBLANES, cc), F32)

    @pl.when(s > 0)
    def _():
        ubuf_ref[0:SUBLANES] = carry_ref[b]

    ubuf_ref[SUBLANES:SUBLANES + ts] = u
    u1 = ubuf_ref[SUBLANES - 1:SUBLANES - 1 + ts]
    u2 = ubuf_ref[SUBLANES - 2:SUBLANES - 2 + ts]
    cw = cw_ref[...]
    yconv_ref[0] = (gb * (cw[0:1] * u2 + cw[1:2] * u1 + cw[2:3] * u)).astype(BF16)
    carry_ref[b] = u[ts - SUBLANES:]
    cst_ref[0, 0] = u[ts - (CONV_K - 1):]

    z = jnp.dot(xm, w_ref[:, 3 * cc:], preferred_element_type=F32)
    hw = N_HEADS * HEAD_W
    q_ref[0] = (_rope(z[:, 0:hw], tab_ref) * Q_SCALE).astype(BF16)
    k_ref[0] = _rope(z[:, hw:2 * hw], tab_ref)
    v_ref[0] = z[:, 2 * hw:]


def _prompt_in(x, mod, w_in, conv_w, tab, ts):
    bsz, seq, d = x.shape
    cc = conv_w.shape[1]
    hw = N_HEADS * HEAD_W
    n_in = w_in.shape[1]
    tok = lambda s, b: (b, s, 0)
    kern = functools.partial(_prompt_in_kernel, ts=ts, cc=cc)
    return pl.pallas_call(
        kern,
        grid=(seq // ts, bsz),
        in_specs=[pl.BlockSpec((1, ts, d), tok),
                  pl.BlockSpec((1, N_MOD, d), lambda s, b: (b, 0, 0)),
                  _resident((d, n_in)),
                  _resident((CONV_K, cc)),
                  pl.BlockSpec((3, ts, LANES), lambda s, b: (0, s, 0))],
        out_specs=[pl.BlockSpec((1, ts, cc), tok),
                   pl.BlockSpec((1, ts, hw), tok),
                   pl.BlockSpec((1, ts, hw), tok),
                   pl.BlockSpec((1, ts, N_HEADS * V_HEAD), tok),
                   pl.BlockSpec((1, 1, CONV_K - 1, cc), lambda s, b: (s, b, 0, 0))],
        out_shape=[jax.ShapeDtypeStruct((bsz, seq, cc), BF16),
                   jax.ShapeDtypeStruct((bsz, seq, hw), BF16),
                   jax.ShapeDtypeStruct((bsz, seq, hw), F32),
                   jax.ShapeDtypeStruct((bsz, seq, N_HEADS * V_HEAD), F32),
                   jax.ShapeDtypeStruct((seq // ts, bsz, CONV_K - 1, cc), F32)],
        scratch_shapes=[pltpu.VMEM((bsz, SUBLANES, cc), F32),
                        pltpu.VMEM((ts + SUBLANES, cc), F32)],
        compiler_params=_params(2),
        name="prompt_in",
    )(x, mod, w_in, conv_w, tab)


def _lambda_value(lam_ref, lam_init):
    lp = lam_ref[...]
    a = jnp.sum(lp[0:1] * lp[1:2], axis=1, keepdims=True)
    b = jnp.sum(lp[2:3] * lp[3:4], axis=1, keepdims=True)
    return jnp.exp(a) - jnp.exp(b) + lam_init


def _head_out(o1, o2, lam, g, lam_init):
    d = o1 - lam * o2
    d = d * lax.rsqrt(jnp.mean(d * d, axis=-1, keepdims=True) + LN_EPS) * g
    return d * (1.0 - lam_init)


def _softmax_step(carry, s, v):
    m, l, acc = carry
    m_new = jnp.maximum(m, jnp.max(s, axis=1, keepdims=True))
    alpha = jnp.exp2(m - m_new)
    p = jnp.exp2(s - m_new)
    l = alpha * l + jnp.sum(p, axis=1, keepdims=True)
    acc = alpha * acc + jnp.dot(p.astype(BF16), v, preferred_element_type=F32)
    return m_new, l, acc


def _prompt_attn_kernel(q_ref, k_ref, v_ref, lam_ref, g_ref, o_ref, *, seq, tq, tk, lam_init):
    lam = _lambda_value(lam_ref, lam_init)
    lane = lax.broadcasted_iota(jnp.int32, (tq, HEAD_W), 1)
    nt = (((1,), (1,)), ((), ()))

    for qi in range(seq // tq):
        q = q_ref[0, qi * tq:(qi + 1) * tq, :]
        zero = jnp.zeros_like(q)
        qq = jnp.concatenate([jnp.where(lane < QK_HEAD, q, zero), jnp.where(lane >= QK_HEAD, q, zero)], axis=0)

        def tile(k0, width, carry, masked):
            k = k_ref[0, pl.ds(k0, width), :].astype(BF16)
            v = v_ref[0, pl.ds(k0, width), :].astype(BF16)
            s = lax.dot_general(qq, k, nt, preferred_element_type=F32)
            if masked:
                row = lax.broadcasted_iota(jnp.int32, (2 * tq, width), 0)
                col = lax.broadcasted_iota(jnp.int32, (2 * tq, width), 1)
                q_pos = jnp.where(row >= tq, row - tq, row)
                s = jnp.where(col <= q_pos, s, NEG_INF)
            return _softmax_step(carry, s, v)

        carry = (jnp.full((2 * tq, 1), NEG_INF, F32), jnp.zeros((2 * tq, 1), F32), jnp.zeros((2 * tq, V_HEAD), F32))
        n_full = (qi * tq) // tk
        if n_full:
            carry = lax.fori_loop(
                0, n_full, lambda i, c: tile(pl.multiple_of(i * tk, tk), tk, c, False), carry)
        carry = tile(qi * tq, tq, carry, True)
        _, l, acc = carry
        o = acc / l
        y = _head_out(o[:tq], o[tq:], lam, g_ref[...], lam_init)
        o_ref[0, qi * tq:(qi + 1) * tq, :] = y.astype(BF16)


def _prompt_attn(q, k, v, lam_params, g, lam_init, tq, tk):
    bsz, seq, _ = q.shape
    blk = pl.BlockSpec((1, seq, HEAD_W), lambda b, h: (b, 0, h))
    kern = functools.partial(_prompt_attn_kernel, seq=seq, tq=tq, tk=tk, lam_init=lam_init)
    return pl.pallas_call(
        kern,
        grid=(bsz, N_HEADS),
        in_specs=[blk, blk, blk,
                  pl.BlockSpec((4, QK_HEAD), lambda b, h: (0, 0)),
                  pl.BlockSpec((1, V_HEAD), lambda b, h: (0, 0))],
        out_specs=blk,
        out_shape=jax.ShapeDtypeStruct((bsz, seq, N_HEADS * V_HEAD), BF16),
        compiler_params=_params(2),
        name="prompt_attn",
    )(q, k, v, lam_params, g)


def _finish_math(x, yc, ya, gate1, shift2, scale2, gate2, wo_ref, ln1g, ln1b, wg_ref, wu_ref, wd_ref,
                 ln2g, ln2b, alpha):
    cc = yc.shape[1]
    y = (jnp.dot(yc, wo_ref[0:cc], preferred_element_type=F32)
         + jnp.dot(ya, wo_ref[cc:], preferred_element_type=F32))
    x1 = _layer_norm(alpha * x + gate1 * y, ln1g, ln1b)
    hf = (x1 * (1.0 + scale2) + shift2).astype(BF16)
    g = jnp.dot(hf, wg_ref[...], preferred_element_type=F32)
    u = jnp.dot(hf, wu_ref[...], preferred_element_type=F32)
    act = (jax.nn.silu(g) * u).astype(BF16)
    f = jnp.dot(act, wd_ref[...], preferred_element_type=F32)
    return _layer_norm(alpha * x1 + gate2 * f, ln2g, ln2b)


def _prompt_finish_kernel(x_ref, yc_ref, ya_ref, mod_ref, wo_ref, ln1_ref, wg_ref, wu_ref, wd_ref, ln2_ref,
                          o_ref, *, alpha):
    mod = mod_ref[0]
    o_ref[0] = _finish_math(x_ref[0], yc_ref[0], ya_ref[0], mod[2:3], mod[3:4], mod[4:5], mod[5:6],
                            wo_ref, ln1_ref[0:1], ln1_ref[1:2], wg_ref, wu_ref, wd_ref,
                            ln2_ref[0:1], ln2_ref[1:2], alpha)


def _sample_finish_kernel(x_ref, yc_ref, o1_ref, o2_ref, lam_ref, g_ref, mod_ref, wo_ref, ln1_ref,
                          wg_ref, wu_ref, wd_ref, ln2_ref, o_ref, *, alpha, lam_init):
    lam = _lambda_value(lam_ref, lam_init)
    o1, o2 = o1_ref[...], o2_ref[...]
    ya = jnp.concatenate(
        [_head_out(o1[:, h * V_HEAD:(h + 1) * V_HEAD], o2[:, h * V_HEAD:(h + 1) * V_HEAD], lam, g_ref[...], lam_init)
         for h in range(N_HEADS)], axis=1).astype(BF16)
    o_ref[...] = _finish_math(x_ref[...], yc_ref[...], ya, mod_ref[2], mod_ref[3], mod_ref[4], mod_ref[5],
                              wo_ref, ln1_ref[0:1], ln1_ref[1:2], wg_ref, wu_ref, wd_ref,
                              ln2_ref[0:1], ln2_ref[1:2], alpha)


def _prompt_finish(x, yc, ya, mod, w_out, ln1, w_gate, w_up, w_down, ln2, alpha, tm):
    bsz, seq, d = x.shape
    cc, d_ff = yc.shape[2], w_gate.shape[1]
    tok = lambda b, s: (b, s, 0)
    kern = functools.partial(_prompt_finish_kernel, alpha=alpha)
    return pl.pallas_call(
        kern,
        grid=(bsz, seq // tm),
        in_specs=[pl.BlockSpec((1, tm, d), tok),
                  pl.BlockSpec((1, tm, cc), tok),
                  pl.BlockSpec((1, tm, d - cc), tok),
                  pl.BlockSpec((1, N_MOD, d), lambda b, s: (b, 0, 0)),
                  _resident((d, d)), _resident((2, d)),
                  _resident((d, d_ff)), _resident((d, d_ff)), _resident((d_ff, d)), _resident((2, d))],
        out_specs=pl.BlockSpec((1, tm, d), tok),
        out_shape=jax.ShapeDtypeStruct((bsz, seq, d), F32),
        compiler_params=_params(2),
        name="prompt_finish",
    )(x, yc, ya, mod, w_out, ln1, w_gate, w_up, w_down, ln2)


def _sample_finish(x, yc, o1, o2, lam_params, g, mod_rows, w_out, ln1, w_gate, w_up, w_down, ln2, alpha, lam_init):
    n, d = x.shape
    kern = functools.partial(_sample_finish_kernel, alpha=alpha, lam_init=lam_init)
    args = (x, yc, o1, o2, lam_params, g, mod_rows, w_out, ln1, w_gate, w_up, w_down, ln2)
    return pl.pallas_call(
        kern,
        grid=(1,),
        in_specs=[_resident(a.shape) for a in args],
        out_specs=pl.BlockSpec((n, d), lambda i: (0, 0)),
        out_shape=jax.ShapeDtypeStruct((n, d), F32),
        compiler_params=_params(1),
        name="sample_finish",
    )(*args)


def _sample_in_kernel(x_ref, mod_ref, w_ref, cw_ref, st1_ref, st2_ref, tab_ref,
                      yconv_ref, q_ref, k_ref, v_ref, u_ref, *, t_len, cc):
    xm = (x_ref[...] * (1.0 + mod_ref[1]) + mod_ref[0]).astype(BF16)
    z = jnp.dot(xm, w_ref[...], preferred_element_type=F32)
    h, gb, gc = z[:, 0:cc], z[:, cc:2 * cc], z[:, 2 * cc:3 * cc]
    u = gc * h
    n = u.shape[0]
    t = lax.broadcasted_iota(jnp.int32, (n, cc), 0) % t_len
    u1 = jnp.where(t >= 1, pltpu.roll(u, 1, 0), st1_ref[...])
    u2 = jnp.where(t >= 2, pltpu.roll(u, 2, 0), st2_ref[...])
    cw = cw_ref[...]
    yconv_ref[...] = (gb * (cw[0:1] * u2 + cw[1:2] * u1 + cw[2:3] * u)).astype(BF16)
    u_ref[...] = u
    hw = N_HEADS * HEAD_W
    q_ref[...] = (_rope(z[:, 3 * cc:3 * cc + hw], tab_ref) * Q_SCALE).astype(BF16)
    k_ref[...] = _rope(z[:, 3 * cc + hw:3 * cc + 2 * hw], tab_ref)
    v_ref[...] = z[:, 3 * cc + 2 * hw:]


def _sample_in(x, mod_rows, w_in, conv_w, st1, st2, tab, t_len):
    n, _ = x.shape
    cc = conv_w.shape[1]
    hw = N_HEADS * HEAD_W
    kern = functools.partial(_sample_in_kernel, t_len=t_len, cc=cc)
    args = (x, mod_rows, w_in, conv_w, st1, st2, tab)
    out_shapes = [jax.ShapeDtypeStruct((n, cc), BF16), jax.ShapeDtypeStruct((n, hw), BF16),
                  jax.ShapeDtypeStruct((n, hw), F32), jax.ShapeDtypeStruct((n, N_HEADS * V_HEAD), F32),
                  jax.ShapeDtypeStruct((n, cc), F32)]
    return pl.pallas_call(
        kern,
        grid=(1,),
        in_specs=[_resident(a.shape) for a in args],
        out_specs=[pl.BlockSpec(o.shape, lambda i: (0, 0)) for o in out_shapes],
        out_shape=out_shapes,
        compiler_params=_params(1),
        name="sample_in",
    )(*args)


def _sample_attn_kernel(pt_ref, wq_ref, kn_ref, vn_ref, ck_ref, cv_ref, o_ref, kbuf, vbuf, sem,
                        *, n_pages, pages_per_chunk, page_base, t_len):
    b = pl.program_id(0)
    n_chunks = n_pages // pages_per_chunk
    page_rows = ck_ref.shape[1]
    nt = (((1,), (1,)), ((), ()))

    def copies(chunk, slot):
        out = []
        for g in range(pages_per_chunk):
            pg = pt_ref[b, chunk * pages_per_chunk + g] + page_base
            dst = pl.ds(g * page_rows, page_rows)
            out.append(pltpu.make_async_copy(ck_ref.at[pg], kbuf.at[slot, dst], sem.at[0, slot]))
            out.append(pltpu.make_async_copy(cv_ref.at[pg], vbuf.at[slot, dst], sem.at[1, slot]))
        return out

    for c in copies(0, 0):
        c.start()

    wq = wq_ref[0]
    rows = wq.shape[0]
    row_head = lax.broadcasted_iota(jnp.int32, (rows, page_rows), 0) // (2 * t_len)
    col_head = lax.broadcasted_iota(jnp.int32, (rows, page_rows), 1) % N_HEADS
    own_head = row_head == col_head
    head_bias = jnp.where(own_head, 0.0, NEG_INF)
    chunk_bias = jnp.concatenate([head_bias] * pages_per_chunk, axis=1)

    def chunk_step(c, carry):
        slot = c % 2

        @pl.when(c + 1 < n_chunks)
        def _():
            for cp in copies(c + 1, 1 - slot):
                cp.start()

        for cp in copies(c, slot):
            cp.wait()
        s = lax.dot_general(wq, kbuf[slot].astype(BF16), nt, preferred_element_type=F32) + chunk_bias
        return _softmax_step(carry, s, vbuf[slot].astype(BF16))

    carry = (jnp.full((rows, 1), NEG_INF, F32), jnp.zeros((rows, 1), F32), jnp.zeros((rows, V_HEAD), F32))
    carry = lax.fori_loop(0, n_chunks, chunk_step, carry)

    s = lax.dot_general(wq, kn_ref[0].astype(BF16), nt, preferred_element_type=F32)
    new_rows = kn_ref.shape[1]
    q_step = lax.broadcasted_iota(jnp.int32, (rows, new_rows), 0) % t_len
    k_step = lax.broadcasted_iota(jnp.int32, (rows, new_rows), 1) // N_HEADS
    s = jnp.where(own_head[:, :new_rows] & (k_step <= q_step), s, NEG_INF)
    _, l, acc = _softmax_step(carry, s, vn_ref[0].astype(BF16))
    o_ref[0] = acc / l


def _sample_attn(page_table, wq, k_new, v_new, cache_k, cache_v, page_base, t_len):
    bsz, n_pages = page_table.shape
    _, page_rows, width = cache_k.shape
    rows = wq.shape[1]
    g = _largest_divisor(n_pages, 8)
    kern = functools.partial(_sample_attn_kernel, n_pages=n_pages, pages_per_chunk=g,
                             page_base=page_base, t_len=t_len)
    per_batch = lambda r: pl.BlockSpec((1, r, width), lambda b, pt: (b, 0, 0))
    grid_spec = pltpu.PrefetchScalarGridSpec(
        num_scalar_prefetch=1,
        grid=(bsz,),
        in_specs=[per_batch(rows), per_batch(k_new.shape[1]), per_batch(v_new.shape[1]),
                  pl.BlockSpec(memory_space=pl.ANY),
                  pl.BlockSpec(memory_space=pl.ANY)],
        out_specs=per_batch(rows),
        scratch_shapes=[pltpu.VMEM((2, g * page_rows, width), F32),
                        pltpu.VMEM((2, g * page_rows, width), F32),
                        pltpu.SemaphoreType.DMA((2, 2))],
    )
    return pl.pallas_call(
        kern,
        grid_spec=grid_spec,
        out_shape=jax.ShapeDtypeStruct((bsz, rows, width), F32),
        compiler_params=_params(1),
        name="sample_attn",
    )(page_table, wq, k_new, v_new, cache_k, cache_v)


def _query_rows(q, bsz, t_len):
    q5 = q.reshape(bsz, t_len, N_HEADS, 2, QK_HEAD).transpose(0, 2, 3, 1, 4)
    eye_c = jnp.eye(2, dtype=q.dtype)
    w = q5[:, :, :, :, None, :] * eye_c[None, None, :, None, :, None]
    return w.reshape(bsz, N_HEADS * 2 * t_len, HEAD_W)


def _token_head_rows(a, bsz, t_len, rows):
    a = a.reshape(bsz, t_len * N_HEADS, a.shape[1] // N_HEADS)
    return jnp.pad(a, ((0, 0), (0, rows - t_len * N_HEADS), (0, 0)))


def _component_outputs(o, bsz, t_len):
    o5 = o.reshape(bsz, N_HEADS, 2, t_len, V_HEAD).transpose(2, 0, 3, 1, 4)
    o5 = o5.reshape(2, bsz * t_len, N_HEADS * V_HEAD)
    return o5[0], o5[1]


def kernel(x_prompt, x_sample, c_prompt, c_sample, cache_k, cache_v, state_conv, page_table, w_ada, b_ada, w_in, conv_w, lambda_q1, lambda_k1, lambda_q2, lambda_k2, subln_g, w_out, ln1_g, ln1_b, w_gate, w_up, w_down, ln2_g, ln2_b):
    bp, s_len, d = x_prompt.shape
    bs, t_len, _ = x_sample.shape
    depth, n_pool, page = cache_k.shape[:3]
    past = page_table.shape[1] * page
    cc = conv_w.shape[2]
    alpha = (2.0 * depth) ** 0.25
    assert (bs * t_len) % SUBLANES == 0 and t_len <= page

    ts = min(512, s_len)
    tq = min(512, s_len)
    tm = min(512, s_len)

    tab_p = _rope_tables(jnp.arange(s_len))
    tab_s = jnp.tile(_rope_tables(past + jnp.arange(t_len)), (1, bs, 1))
    ck = cache_k.reshape(depth * n_pool, page * N_HEADS, HEAD_W)
    cv = cache_v.reshape(depth * n_pool, page * N_HEADS, V_HEAD)
    c_all = jnp.concatenate([c_prompt, c_sample], axis=0)

    xp, xs = x_prompt, x_sample.reshape(bs * t_len, d)
    kp_l, vp_l, cp_l, ks_l, vs_l, cs_l = [], [], [], [], [], []
    for l in range(depth):
        lam_init = 0.8 - 0.6 * math.exp(-0.3 * l)
        w_in_b, w_out_b = w_in[l].astype(BF16), w_out[l].astype(BF16)
        w_gate_b, w_up_b, w_down_b = w_gate[l].astype(BF16), w_up[l].astype(BF16), w_down[l].astype(BF16)
        lam_params = jnp.stack([lambda_q1[l], lambda_k1[l], lambda_q2[l], lambda_k2[l]])
        g = subln_g[l].reshape(1, V_HEAD)
        ln1 = jnp.stack([ln1_g[l], ln1_b[l]])
        ln2 = jnp.stack([ln2_g[l], ln2_b[l]])

        mod = _adaln(c_all, w_ada[l], b_ada[l]).reshape(bp + bs, N_MOD, d)
        mod_p = mod[:bp]
        mod_s = jnp.repeat(mod[bp:].transpose(1, 0, 2), t_len, axis=1)

        yconv, q, k, v, cst = _prompt_in(xp, mod_p, w_in_b, conv_w[l], tab_p, ts)
        yattn = _prompt_attn(q, k, v, lam_params, g, lam_init, tq, tq)
        xp = _prompt_finish(xp, yconv, yattn, mod_p, w_out_b, ln1, w_gate_b, w_up_b, w_down_b, ln2, alpha, tm)
        kp_l.append(k.reshape(bp, s_len, N_HEADS, HEAD_W))
        vp_l.append(v.reshape(bp, s_len, N_HEADS, V_HEAD))
        cp_l.append(cst[-1])

        st = state_conv[l]
        zero = jnp.zeros((bs, 1, cc), F32)
        st1 = jnp.concatenate([st[:, 1:2]] + [zero] * (t_len - 1), axis=1).reshape(bs * t_len, cc)
        st2 = jnp.concatenate([st[:, 0:1], st[:, 1:2]] + [zero] * (t_len - 2), axis=1).reshape(bs * t_len, cc)
        yconv_s, q_s, k_s, v_s, u_s = _sample_in(xs, mod_s, w_in_b, conv_w[l], st1, st2, tab_s, t_len)
        wq = _query_rows(q_s, bs, t_len)
        rows = pl.cdiv(t_len * N_HEADS, LANES) * LANES
        o = _sample_attn(page_table, wq, _token_head_rows(k_s, bs, t_len, rows),
                         _token_head_rows(v_s, bs, t_len, rows), ck, cv, l * n_pool, t_len)
        o1, o2 = _component_outputs(o, bs, t_len)
        xs = _sample_finish(xs, yconv_s, o1, o2, lam_params, g, mod_s, w_out_b, ln1, w_gate_b, w_up_b, w_down_b,
                            ln2, alpha, lam_init)
        ks_l.append(k_s.reshape(bs, t_len, N_HEADS, HEAD_W))
        vs_l.append(v_s.reshape(bs, t_len, N_HEADS, V_HEAD))
        cs_l.append(u_s.reshape(bs, t_len, cc)[:, t_len - (CONV_K - 1):])

    return (xp, xs.reshape(bs, t_len, d), jnp.stack(kp_l), jnp.stack(vp_l), jnp.stack(cp_l),
            jnp.stack(ks_l), jnp.stack(vs_l), jnp.stack(cs_l))
```

```python
import functools
import math

import jax
import jax.numpy as jnp
from jax import lax
from jax.experimental import pallas as pl
from jax.experimental.pallas import tpu as pltpu

F32 = jnp.float32
BF16 = jnp.bfloat16

N_HEADS = 4
QK_HEAD = 64
V_HEAD = 128
HEAD_W = 2 * QK_HEAD
CONV_K = 3
ROT_DIM = QK_HEAD // 4
ROPE_THETA = 500000.0
N_MOD = 6
LN_EPS = 1e-5
LANES = 128
SUBLANES = 8
VMEM_LIMIT_BYTES = 56 * 1024 * 1024
NEG_INF = float("-inf")
Q_SCALE = QK_HEAD ** -0.5 * math.log2(math.e)


def _params(n_axes):
    return pltpu.CompilerParams(dimension_semantics=("arbitrary",) * n_axes,
                                vmem_limit_bytes=VMEM_LIMIT_BYTES)


def _resident(shape):
    return pl.BlockSpec(shape, lambda *_: (0,) * len(shape), pipeline_mode=pl.Buffered(1))


def _largest_divisor(n, cap):
    return max(d for d in range(1, min(n, cap) + 1) if n % d == 0)


def _layer_norm(x, g, b):
    mu = jnp.mean(x, axis=-1, keepdims=True)
    xc = x - mu
    var = jnp.mean(xc * xc, axis=-1, keepdims=True)
    return xc * lax.rsqrt(var + LN_EPS) * g + b


def _adaln_kernel(c_ref, w_ref, b_ref, o_ref):
    a = jax.nn.silu(c_ref[...]).astype(BF16)
    o_ref[...] = jnp.dot(a, w_ref[...].astype(BF16), preferred_element_type=F32) + b_ref[...]


def _adaln(c, w, b):
    n, d = c.shape
    n_out = w.shape[1]
    tn = 768 if n_out % 768 == 0 else n_out
    return pl.pallas_call(
        _adaln_kernel,
        grid=(n_out // tn,),
        in_specs=[pl.BlockSpec((n, d), lambda j: (0, 0)),
                  pl.BlockSpec((d, tn), lambda j: (0, j)),
                  pl.BlockSpec((1, tn), lambda j: (0, j))],
        out_specs=pl.BlockSpec((n, tn), lambda j: (0, j)),
        out_shape=jax.ShapeDtypeStruct((n, n_out), F32),
        compiler_params=_params(1),
        name="adaln",
    )(c, w, b.reshape(1, n_out))


def _rope_tables(pos):
    inv = ROPE_THETA ** (-jnp.arange(0, ROT_DIM, 2, dtype=F32) / ROT_DIM)
    ang = pos.astype(F32)[:, None] * inv[None, :]
    cos, sin = jnp.cos(ang), jnp.sin(ang)
    t, half = pos.shape[0], ROT_DIM // 2
    ones = jnp.ones((t, QK_HEAD - ROT_DIM), F32)
    c = jnp.concatenate([cos, cos, ones], axis=1)
    s_up = jnp.concatenate([-sin, jnp.zeros((t, QK_HEAD - half), F32)], axis=1)
    s_dn = jnp.concatenate([jnp.zeros((t, half), F32), sin, jnp.zeros((t, QK_HEAD - ROT_DIM), F32)], axis=1)
    return jnp.stack([jnp.tile(a, (1, LANES // QK_HEAD)) for a in (c, s_up, s_dn)])


def _rope(x, tab_ref):
    c, s_up, s_dn = tab_ref[0], tab_ref[1], tab_ref[2]
    half = ROT_DIM // 2
    cols = []
    for i in range(x.shape[1] // LANES):
        xc = x[:, i * LANES:(i + 1) * LANES]
        cols.append(xc * c + pltpu.roll(xc, LANES - half, 1) * s_up + pltpu.roll(xc, half, 1) * s_dn)
    return jnp.concatenate(cols, axis=1)


def _prompt_in_kernel(x_ref, mod_ref, w_ref, cw_ref, tab_ref,
                      yconv_ref, q_ref, kb_ref, vb_ref, k_ref, v_ref, cst_ref, carry_ref, ubuf_ref, *, ts, cc):
    s, b = pl.program_id(0), pl.program_id(1)

    @pl.when(s == 0)
    def _():
        carry_ref[b] = jnp.zeros((SUBLANES, cc), F32)

    mod = mod_ref[0]
    xm = (x_ref[0] * (1.0 + mod[1:2]) + mod[0:1]).astype(BF16)

    z = jnp.dot(xm, w_ref[:, 0:3 * cc], preferred_element_type=F32)
    h, gb, gc = z[:, 0:cc], z[:, cc:2 * cc], z[:, 2 * cc:3 * cc]
    u = gc * h

    ubuf_ref[0:SUBLANES] = carry_ref[b]
    ubuf_ref[SUBLANES:SUBLANES + ts] = u
    u1 = ubuf_ref[SUBLANES - 1:SUBLANES - 1 + ts]
    u2 = ubuf_ref[SUBLANES - 2:SUBLANES - 2 + ts]
    cw = cw_ref[...]
    yconv_ref[0] = (gb * (cw[0:1] * u2 + cw[1:2] * u1 + cw[2:3] * u)).astype(BF16)
    carry_ref[b] = u[ts - SUBLANES:]
    cst_ref[0, 0] = u[ts - (CONV_K - 1):]

    z = jnp.dot(xm, w_ref[:, 3 * cc:], preferred_element_type=F32)
    hw = N_HEADS * HEAD_W
    q_ref[0] = (_rope(z[:, 0:hw], tab_ref) * Q_SCALE).astype(BF16)
    k = _rope(z[:, hw:2 * hw], tab_ref)
    v = z[:, 2 * hw:]
    kb_ref[0] = k.astype(BF16)
    vb_ref[0] = v.astype(BF16)
    for h in range(N_HEADS):
        rows = pl.ds(h, ts, stride=N_HEADS)
        k_ref[0, rows, :] = k[:, h * HEAD_W:(h + 1) * HEAD_W]
        v_ref[0, rows, :] = v[:, h * V_HEAD:(h + 1) * V_HEAD]


def _prompt_in(x, mod, w_in, conv_w, tab, ts):
    bsz, seq, d = x.shape
    cc = conv_w.shape[1]
    hw = N_HEADS * HEAD_W
    n_in = w_in.shape[1]
    tok = lambda s, b: (b, s, 0)
    kern = functools.partial(_prompt_in_kernel, ts=ts, cc=cc)
    return pl.pallas_call(
        kern,
        grid=(seq // ts, bsz),
        in_specs=[pl.BlockSpec((1, ts, d), tok),
                  pl.BlockSpec((1, N_MOD, d), lambda s, b: (b, 0, 0)),
                  _resident((d, n_in)),
                  _resident((CONV_K, cc)),
                  pl.BlockSpec((3, ts, LANES), lambda s, b: (0, s, 0))],
        out_specs=[pl.BlockSpec((1, ts, cc), tok),
                   pl.BlockSpec((1, ts, hw), tok),
                   pl.BlockSpec((1, ts, hw), tok),
                   pl.BlockSpec((1, ts, N_HEADS * V_HEAD), tok),
                   pl.BlockSpec((1, ts * N_HEADS, HEAD_W), tok),
                   pl.BlockSpec((1, ts * N_HEADS, V_HEAD), tok),
                   pl.BlockSpec((1, 1, CONV_K - 1, cc), lambda s, b: (s, b, 0, 0))],
        out_shape=[jax.ShapeDtypeStruct((bsz, seq, cc), BF16),
                   jax.ShapeDtypeStruct((bsz, seq, hw), BF16),
                   jax.ShapeDtypeStruct((bsz, seq, hw), BF16),
                   jax.ShapeDtypeStruct((bsz, seq, N_HEADS * V_HEAD), BF16),
                   jax.ShapeDtypeStruct((bsz, seq * N_HEADS, HEAD_W), F32),
                   jax.ShapeDtypeStruct((bsz, seq * N_HEADS, V_HEAD), F32),
                   jax.ShapeDtypeStruct((seq // ts, bsz, CONV_K - 1, cc), F32)],
        scratch_shapes=[pltpu.VMEM((bsz, SUBLANES, cc), F32),
                        pltpu.VMEM((ts + SUBLANES, cc), F32)],
        compiler_params=_params(2),
        name="prompt_in",
    )(x, mod, w_in, conv_w, tab)


def _lambda_value(lam_ref, lam_init):
    lp = lam_ref[...]
    a = jnp.sum(lp[0:1] * lp[1:2], axis=1, keepdims=True)
    b = jnp.sum(lp[2:3] * lp[3:4], axis=1, keepdims=True)
    return jnp.exp(a) - jnp.exp(b) + lam_init


def _head_out(o1, o2, lam, g, lam_init):
    d = o1 - lam * o2
    d = d * lax.rsqrt(jnp.mean(d * d, axis=-1, keepdims=True) + LN_EPS) * g
    return d * (1.0 - lam_init)


def _softmax_step(carry, s, v):
    m, l, acc = carry
    m_new = jnp.maximum(m, jnp.max(s, axis=1, keepdims=True))
    alpha = jnp.exp2(m - m_new)
    p = jnp.exp2(s - m_new)
    l = alpha * l + jnp.sum(p, axis=1, keepdims=True)
    acc = alpha * acc + jnp.dot(p.astype(BF16), v, preferred_element_type=F32)
    return m_new, l, acc


def _prompt_attn_kernel(q_ref, k_ref, v_ref, lam_ref, g_ref, o_ref, *, seq, tq, lam_init):
    lam = _lambda_value(lam_ref, lam_init)
    lane = lax.broadcasted_iota(jnp.int32, (tq, HEAD_W), 1)
    nt = (((1,), (1,)), ((), ()))

    def stacked_queries(qi):
        q = q_ref[0, qi * tq:(qi + 1) * tq, :]
        zero = jnp.zeros_like(q)
        return jnp.concatenate([jnp.where(lane < QK_HEAD, q, zero), jnp.where(lane >= QK_HEAD, q, zero)], axis=0)

    def scores(qq, qi, ki):
        s = lax.dot_general(qq, k_ref[0, ki * tq:(ki + 1) * tq, :], nt, preferred_element_type=F32)
        if ki == qi:
            row = lax.broadcasted_iota(jnp.int32, s.shape, 0)
            col = lax.broadcasted_iota(jnp.int32, s.shape, 1)
            s = jnp.where(col <= jnp.where(row >= tq, row - tq, row), s, NEG_INF)
        return s

    def finish(qi, carry):
        _, l, acc = carry
        o = acc / l
        y = _head_out(o[:tq], o[tq:], lam, g_ref[...], lam_init)
        o_ref[0, qi * tq:(qi + 1) * tq, :] = y.astype(BF16)

    for qi in range(seq // tq):
        qq = stacked_queries(qi)
        carry = (jnp.full((2 * tq, 1), NEG_INF, F32), jnp.zeros((2 * tq, 1), F32), jnp.zeros((2 * tq, V_HEAD), F32))
        for ki in range(qi + 1):
            carry = _softmax_step(carry, scores(qq, qi, ki), v_ref[0, ki * tq:(ki + 1) * tq, :])
        finish(qi, carry)


def _prompt_attn(q, k, v, lam_params, g, lam_init, tq):
    bsz, seq, _ = q.shape
    blk = pl.BlockSpec((1, seq, HEAD_W), lambda b, h: (b, 0, h))
    kern = functools.partial(_prompt_attn_kernel, seq=seq, tq=tq, lam_init=lam_init)
    return pl.pallas_call(
        kern,
        grid=(bsz, N_HEADS),
        in_specs=[blk, blk, blk,
                  pl.BlockSpec((4, QK_HEAD), lambda b, h: (0, 0)),
                  pl.BlockSpec((1, V_HEAD), lambda b, h: (0, 0))],
        out_specs=blk,
        out_shape=jax.ShapeDtypeStruct((bsz, seq, N_HEADS * V_HEAD), BF16),
        compiler_params=_params(2),
        name="prompt_attn",
    )(q, k, v, lam_params, g)


def _finish_math(x, yc, ya, gate1, shift2, scale2, gate2, wo_ref, ln1g, ln1b, wg_ref, wu_ref, wd_ref,
                 ln2g, ln2b, alpha):
    cc = yc.shape[1]
    y = (jnp.dot(yc, wo_ref[0:cc], preferred_element_type=F32)
         + jnp.dot(ya, wo_ref[cc:], preferred_element_type=F32))
    x1 = _layer_norm(alpha * x + gate1 * y, ln1g, ln1b)
    hf = (x1 * (1.0 + scale2) + shift2).astype(BF16)
    g = jnp.dot(hf, wg_ref[...], preferred_element_type=F32)
    u = jnp.dot(hf, wu_ref[...], preferred_element_type=F32)
    act = (jax.nn.silu(g) * u).astype(BF16)
    f = jnp.dot(act, wd_ref[...], preferred_element_type=F32)
    return _layer_norm(alpha * x1 + gate2 * f, ln2g, ln2b)


def _prompt_finish_kernel(x_ref, yc_ref, ya_ref, mod_ref, wo_ref, ln1_ref, wg_ref, wu_ref, wd_ref, ln2_ref,
                          o_ref, *, alpha):
    mod = mod_ref[0]
    o_ref[0] = _finish_math(x_ref[0], yc_ref[0], ya_ref[0], mod[2:3], mod[3:4], mod[4:5], mod[5:6],
                            wo_ref, ln1_ref[0:1], ln1_ref[1:2], wg_ref, wu_ref, wd_ref,
                            ln2_ref[0:1], ln2_ref[1:2], alpha)


def _sample_finish_kernel(x_ref, yc_ref, o1_ref, o2_ref, lam_ref, g_ref, mod_ref, wo_ref, ln1_ref,
                          wg_ref, wu_ref, wd_ref, ln2_ref, o_ref, *, alpha, lam_init):
    lam = _lambda_value(lam_ref, lam_init)
    o1, o2 = o1_ref[...], o2_ref[...]
    ya = jnp.concatenate(
        [_head_out(o1[:, h * V_HEAD:(h + 1) * V_HEAD], o2[:, h * V_HEAD:(h + 1) * V_HEAD], lam, g_ref[...], lam_init)
         for h in range(N_HEADS)], axis=1).astype(BF16)
    o_ref[...] = _finish_math(x_ref[...], yc_ref[...], ya, mod_ref[2], mod_ref[3], mod_ref[4], mod_ref[5],
                              wo_ref, ln1_ref[0:1], ln1_ref[1:2], wg_ref, wu_ref, wd_ref,
                              ln2_ref[0:1], ln2_ref[1:2], alpha)


def _prompt_finish(x, yc, ya, mod, w_out, ln1, w_gate, w_up, w_down, ln2, alpha, tm):
    bsz, seq, d = x.shape
    cc, d_ff = yc.shape[2], w_gate.shape[1]
    tok = lambda b, s: (b, s, 0)
    kern = functools.partial(_prompt_finish_kernel, alpha=alpha)
    return pl.pallas_call(
        kern,
        grid=(bsz, seq // tm),
        in_specs=[pl.BlockSpec((1, tm, d), tok),
                  pl.BlockSpec((1, tm, cc), tok),
                  pl.BlockSpec((1, tm, d - cc), tok),
                  pl.BlockSpec((1, N_MOD, d), lambda b, s: (b, 0, 0)),
                  _resident((d, d)), _resident((2, d)),
                  _resident((d, d_ff)), _resident((d, d_ff)), _resident((d_ff, d)), _resident((2, d))],
        out_specs=pl.BlockSpec((1, tm, d), tok),
        out_shape=jax.ShapeDtypeStruct((bsz, seq, d), F32),
        compiler_params=_params(2),
        name="prompt_finish",
    )(x, yc, ya, mod, w_out, ln1, w_gate, w_up, w_down, ln2)


def _sample_finish(x, yc, o1, o2, lam_params, g, mod_rows, w_out, ln1, w_gate, w_up, w_down, ln2, alpha, lam_init):
    n, d = x.shape
    kern = functools.partial(_sample_finish_kernel, alpha=alpha, lam_init=lam_init)
    args = (x, yc, o1, o2, lam_params, g, mod_rows, w_out, ln1, w_gate, w_up, w_down, ln2)
    return pl.pallas_call(
        kern,
        grid=(1,),
        in_specs=[_resident(a.shape) for a in args],
        out_specs=pl.BlockSpec((n, d), lambda i: (0, 0)),
        out_shape=jax.ShapeDtypeStruct((n, d), F32),
        compiler_params=_params(1),
        name="sample_finish",
    )(*args)


def _sample_in_kernel(x_ref, mod_ref, w_ref, cw_ref, st1_ref, st2_ref, tab_ref,
                      yconv_ref, q_ref, k_ref, v_ref, u_ref, *, t_len, cc):
    xm = (x_ref[...] * (1.0 + mod_ref[1]) + mod_ref[0]).astype(BF16)
    z = jnp.dot(xm, w_ref[...], preferred_element_type=F32)
    h, gb, gc = z[:, 0:cc], z[:, cc:2 * cc], z[:, 2 * cc:3 * cc]
    u = gc * h
    n = u.shape[0]
    t = lax.broadcasted_iota(jnp.int32, (n, cc), 0) % t_len
    u1 = jnp.where(t >= 1, pltpu.roll(u, 1, 0), st1_ref[...])
    u2 = jnp.where(t >= 2, pltpu.roll(u, 2, 0), st2_ref[...])
    cw = cw_ref[...]
    yconv_ref[...] = (gb * (cw[0:1] * u2 + cw[1:2] * u1 + cw[2:3] * u)).astype(BF16)
    u_ref[...] = u
    hw = N_HEADS * HEAD_W
    q_ref[...] = (_rope(z[:, 3 * cc:3 * cc + hw], tab_ref) * Q_SCALE).astype(BF16)
    k_ref[...] = _rope(z[:, 3 * cc + hw:3 * cc + 2 * hw], tab_ref)
    v_ref[...] = z[:, 3 * cc + 2 * hw:]


def _sample_in(x, mod_rows, w_in, conv_w, st1, st2, tab, t_len):
    n, _ = x.shape
    cc = conv_w.shape[1]
    hw = N_HEADS * HEAD_W
    kern = functools.partial(_sample_in_kernel, t_len=t_len, cc=cc)
    args = (x, mod_rows, w_in, conv_w, st1, st2, tab)
    out_shapes = [jax.ShapeDtypeStruct((n, cc), BF16), jax.ShapeDtypeStruct((n, hw), BF16),
                  jax.ShapeDtypeStruct((n, hw), F32), jax.ShapeDtypeStruct((n, N_HEADS * V_HEAD), F32),
                  jax.ShapeDtypeStruct((n, cc), F32)]
    return pl.pallas_call(
        kern,
        grid=(1,),
        in_specs=[_resident(a.shape) for a in args],
        out_specs=[pl.BlockSpec(o.shape, lambda i: (0, 0)) for o in out_shapes],
        out_shape=out_shapes,
        compiler_params=_params(1),
        name="sample_in",
    )(*args)


def _sample_attn_kernel(pt_ref, wq_ref, kn_ref, vn_ref, ck_ref, cv_ref, o_ref, kbuf, vbuf, sem,
                        *, n_pages, pages_per_chunk, page_base, t_len):
    b = pl.program_id(0)
    n_batches = pl.num_programs(0)
    n_chunks = n_pages // pages_per_chunk
    page_rows = ck_ref.shape[1]
    nt = (((1,), (1,)), ((), ()))

    def copies(batch, chunk, slot):
        out = []
        for g in range(pages_per_chunk):
            pg = pt_ref[batch, chunk * pages_per_chunk + g] + page_base
            dst = pl.ds(g * page_rows, page_rows)
            out.append((pltpu.make_async_copy(ck_ref.at[pg], kbuf.at[slot, dst], sem.at[0, slot]),
                        pltpu.make_async_copy(cv_ref.at[pg], vbuf.at[slot, dst], sem.at[1, slot])))
        return out

    def start(batch, chunk, slot):
        for k_copy, v_copy in copies(batch, chunk, slot):
            k_copy.start(priority=0)
            v_copy.start(priority=1)

    @pl.when(b == 0)
    def _():
        start(0, 0, 0)

    wq = wq_ref[0]
    rows = wq.shape[0]
    row_head = lax.broadcasted_iota(jnp.int32, (rows, page_rows), 0) // (2 * t_len)
    col_head = lax.broadcasted_iota(jnp.int32, (rows, page_rows), 1) % N_HEADS
    own_head = row_head == col_head
    head_bias = jnp.where(own_head, 0.0, NEG_INF)
    chunk_bias = jnp.concatenate([head_bias] * pages_per_chunk, axis=1)

    def chunk_step(c, carry):
        slot = (b * n_chunks + c) % 2
        last = c + 1 == n_chunks

        @pl.when(jnp.logical_or(jnp.logical_not(last), b + 1 < n_batches))
        def _():
            start(jnp.where(last, b + 1, b), jnp.where(last, 0, c + 1), 1 - slot)

        for k_copy, v_copy in copies(b, c, slot):
            k_copy.wait()
            v_copy.wait()
        s = lax.dot_general(wq, kbuf[slot].astype(BF16), nt, preferred_element_type=F32) + chunk_bias
        return _softmax_step(carry, s, vbuf[slot].astype(BF16))

    carry = (jnp.full((rows, 1), NEG_INF, F32), jnp.zeros((rows, 1), F32), jnp.zeros((rows, V_HEAD), F32))
    carry = lax.fori_loop(0, n_chunks, chunk_step, carry)

    s = lax.dot_general(wq, kn_ref[0].astype(BF16), nt, preferred_element_type=F32)
    new_rows = kn_ref.shape[1]
    q_step = lax.broadcasted_iota(jnp.int32, (rows, new_rows), 0) % t_len
    k_step = lax.broadcasted_iota(jnp.int32, (rows, new_rows), 1) // N_HEADS
    s = jnp.where(own_head[:, :new_rows] & (k_step <= q_step), s, NEG_INF)
    _, l, acc = _softmax_step(carry, s, vn_ref[0].astype(BF16))
    o_ref[0] = acc / l


def _sample_attn(page_table, wq, k_new, v_new, cache_k, cache_v, page_base, t_len):
    bsz, n_pages = page_table.shape
    _, page_rows, width = cache_k.shape
    rows = wq.shape[1]
    g = _largest_divisor(n_pages, 8)
    kern = functools.partial(_sample_attn_kernel, n_pages=n_pages, pages_per_chunk=g,
                             page_base=page_base, t_len=t_len)
    per_batch = lambda r: pl.BlockSpec((1, r, width), lambda b, pt: (b, 0, 0))
    grid_spec = pltpu.PrefetchScalarGridSpec(
        num_scalar_prefetch=1,
        grid=(bsz,),
        in_specs=[per_batch(rows), per_batch(k_new.shape[1]), per_batch(v_new.shape[1]),
                  pl.BlockSpec(memory_space=pl.ANY),
                  pl.BlockSpec(memory_space=pl.ANY)],
        out_specs=per_batch(rows),
        scratch_shapes=[pltpu.VMEM((2, g * page_rows, width), F32),
                        pltpu.VMEM((2, g * page_rows, width), F32),
                        pltpu.SemaphoreType.DMA((2, 2))],
    )
    return pl.pallas_call(
        kern,
        grid_spec=grid_spec,
        out_shape=jax.ShapeDtypeStruct((bsz, rows, width), F32),
        compiler_params=_params(1),
        name="sample_attn",
    )(page_table, wq, k_new, v_new, cache_k, cache_v)


def _query_rows(q, bsz, t_len):
    q5 = q.reshape(bsz, t_len, N_HEADS, 2, QK_HEAD).transpose(0, 2, 3, 1, 4)
    eye_c = jnp.eye(2, dtype=q.dtype)
    w = q5[:, :, :, :, None, :] * eye_c[None, None, :, None, :, None]
    return w.reshape(bsz, N_HEADS * 2 * t_len, HEAD_W)


def _token_head_rows(a, bsz, t_len, rows):
    a = a.reshape(bsz, t_len * N_HEADS, a.shape[1] // N_HEADS)
    return jnp.pad(a, ((0, 0), (0, rows - t_len * N_HEADS), (0, 0)))


def _component_outputs(o, bsz, t_len):
    o5 = o.reshape(bsz, N_HEADS, 2, t_len, V_HEAD).transpose(2, 0, 3, 1, 4)
    o5 = o5.reshape(2, bsz * t_len, N_HEADS * V_HEAD)
    return o5[0], o5[1]


def kernel(x_prompt, x_sample, c_prompt, c_sample, cache_k, cache_v, state_conv, page_table, w_ada, b_ada, w_in, conv_w, lambda_q1, lambda_k1, lambda_q2, lambda_k2, subln_g, w_out, ln1_g, ln1_b, w_gate, w_up, w_down, ln2_g, ln2_b):
    bp, s_len, d = x_prompt.shape
    bs, t_len, _ = x_sample.shape
    depth, n_pool, page = cache_k.shape[:3]
    past = page_table.shape[1] * page
    cc = conv_w.shape[2]
    alpha = (2.0 * depth) ** 0.25
    assert (bs * t_len) % SUBLANES == 0 and t_len <= page

    ts = min(1024, s_len)
    tq = min(512, s_len)
    tm = min(512, s_len)

    tab_p = _rope_tables(jnp.arange(s_len))
    tab_s = jnp.tile(_rope_tables(past + jnp.arange(t_len)), (1, bs, 1))
    ck = cache_k.reshape(depth * n_pool, page * N_HEADS, HEAD_W)
    cv = cache_v.reshape(depth * n_pool, page * N_HEADS, V_HEAD)
    c_all = jnp.concatenate([c_prompt, c_sample], axis=0)

    xp, xs = x_prompt, x_sample.reshape(bs * t_len, d)
    kp_l, vp_l, cp_l, ks_l, vs_l, cs_l = [], [], [], [], [], []
    for l in range(depth):
        lam_init = 0.8 - 0.6 * math.exp(-0.3 * l)
        w_in_b, w_out_b = w_in[l].astype(BF16), w_out[l].astype(BF16)
        w_gate_b, w_up_b, w_down_b = w_gate[l].astype(BF16), w_up[l].astype(BF16), w_down[l].astype(BF16)
        lam_params = jnp.stack([lambda_q1[l], lambda_k1[l], lambda_q2[l], lambda_k2[l]])
        g = subln_g[l].reshape(1, V_HEAD)
        ln1 = jnp.stack([ln1_g[l], ln1_b[l]])
        ln2 = jnp.stack([ln2_g[l], ln2_b[l]])

        mod = _adaln(c_all, w_ada[l], b_ada[l]).reshape(bp + bs, N_MOD, d)
        mod_p = mod[:bp]
        mod_s = jnp.repeat(mod[bp:].transpose(1, 0, 2), t_len, axis=1)

        yconv, q, kb, vb, k, v, cst = _prompt_in(xp, mod_p, w_in_b, conv_w[l], tab_p, ts)
        yattn = _prompt_attn(q, kb, vb, lam_params, g, lam_init, tq)
        xp = _prompt_finish(xp, yconv, yattn, mod_p, w_out_b, ln1, w_gate_b, w_up_b, w_down_b, ln2, alpha, tm)
        kp_l.append(k.reshape(bp, s_len, N_HEADS, HEAD_W))
        vp_l.append(v.reshape(bp, s_len, N_HEADS, V_HEAD))
        cp_l.append(cst[-1])

        st = state_conv[l]
        zero = jnp.zeros((bs, 1, cc), F32)
        st1 = jnp.concatenate([st[:, 1:2]] + [zero] * (t_len - 1), axis=1).reshape(bs * t_len, cc)
        st2 = jnp.concatenate([st[:, 0:1], st[:, 1:2]] + [zero] * (t_len - 2), axis=1).reshape(bs * t_len, cc)
        yconv_s, q_s, k_s, v_s, u_s = _sample_in(xs, mod_s, w_in_b, conv_w[l], st1, st2, tab_s, t_len)
        wq = _query_rows(q_s, bs, t_len)
        rows = pl.cdiv(t_len * N_HEADS, LANES) * LANES
        o = _sample_attn(page_table, wq, _token_head_rows(k_s, bs, t_len, rows),
                         _token_head_rows(v_s, bs, t_len, rows), ck, cv, l * n_pool, t_len)
        o1, o2 = _component_outputs(o, bs, t_len)
        xs = _sample_finish(xs, yconv_s, o1, o2, lam_params, g, mod_s, w_out_b, ln1, w_gate_b, w_up_b, w_down_b,
                            ln2, alpha, lam_init)
        ks_l.append(k_s.reshape(bs, t_len, N_HEADS, HEAD_W))
        vs_l.append(v_s.reshape(bs, t_len, N_HEADS, V_HEAD))
        cs_l.append(u_s.reshape(bs, t_len, cc)[:, t_len - (CONV_K - 1):])

    return (xp, xs.reshape(bs, t_len, d), jnp.stack(kp_l), jnp.stack(vp_l), jnp.stack(cp_l),
            jnp.stack(ks_l), jnp.stack(vs_l), jnp.stack(cs_l))
```

```python
import functools
import math

import jax
import jax.numpy as jnp
from jax import lax
from jax.experimental import pallas as pl
from jax.experimental.pallas import tpu as pltpu

F32 = jnp.float32
BF16 = jnp.bfloat16

N_HEADS = 4
QK_HEAD = 64
V_HEAD = 128
HEAD_W = 2 * QK_HEAD
CONV_K = 3
ROT_DIM = QK_HEAD // 4
ROPE_THETA = 500000.0
N_MOD = 6
LN_EPS = 1e-5
LANES = 128
SUBLANES = 8
VMEM_LIMIT_BYTES = 56 * 1024 * 1024
NEG_INF = float("-inf")
Q_SCALE = QK_HEAD ** -0.5 * math.log2(math.e)


def _params(n_axes):
    return pltpu.CompilerParams(dimension_semantics=("arbitrary",) * n_axes,
                                vmem_limit_bytes=VMEM_LIMIT_BYTES)


def _resident(shape):
    return pl.BlockSpec(shape, lambda *_: (0,) * len(shape), pipeline_mode=pl.Buffered(1))


def _largest_divisor(n, cap):
    return max(d for d in range(1, min(n, cap) + 1) if n % d == 0)


def _layer_norm(x, g, b):
    mu = jnp.mean(x, axis=-1, keepdims=True)
    xc = x - mu
    var = jnp.mean(xc * xc, axis=-1, keepdims=True)
    return xc * lax.rsqrt(var + LN_EPS) * g + b


def _adaln_kernel(c_ref, w_ref, b_ref, o_ref):
    a = jax.nn.silu(c_ref[...]).astype(BF16)
    o_ref[...] = jnp.dot(a, w_ref[...].astype(BF16), preferred_element_type=F32) + b_ref[...]


def _adaln(c, w, b):
    n, d = c.shape
    n_out = w.shape[1]
    tn = 768 if n_out % 768 == 0 else n_out
    return pl.pallas_call(
        _adaln_kernel,
        grid=(n_out // tn,),
        in_specs=[pl.BlockSpec((n, d), lambda j: (0, 0)),
                  pl.BlockSpec((d, tn), lambda j: (0, j)),
                  pl.BlockSpec((1, tn), lambda j: (0, j))],
        out_specs=pl.BlockSpec((n, tn), lambda j: (0, j)),
        out_shape=jax.ShapeDtypeStruct((n, n_out), F32),
        compiler_params=_params(1),
        name="adaln",
    )(c, w, b.reshape(1, n_out))


def _rope_tables(pos):
    inv = ROPE_THETA ** (-jnp.arange(0, ROT_DIM, 2, dtype=F32) / ROT_DIM)
    ang = pos.astype(F32)[:, None] * inv[None, :]
    cos, sin = jnp.cos(ang), jnp.sin(ang)
    t, half = pos.shape[0], ROT_DIM // 2
    ones = jnp.ones((t, QK_HEAD - ROT_DIM), F32)
    c = jnp.concatenate([cos, cos, ones], axis=1)
    s_up = jnp.concatenate([-sin, jnp.zeros((t, QK_HEAD - half), F32)], axis=1)
    s_dn = jnp.concatenate([jnp.zeros((t, half), F32), sin, jnp.zeros((t, QK_HEAD - ROT_DIM), F32)], axis=1)
    return jnp.stack([jnp.tile(a, (1, LANES // QK_HEAD)) for a in (c, s_up, s_dn)])


def _rope(x, tab_ref):
    c, s_up, s_dn = tab_ref[0], tab_ref[1], tab_ref[2]
    half = ROT_DIM // 2
    cols = []
    for i in range(x.shape[1] // LANES):
        xc = x[:, i * LANES:(i + 1) * LANES]
        cols.append(xc * c + pltpu.roll(xc, LANES - half, 1) * s_up + pltpu.roll(xc, half, 1) * s_dn)
    return jnp.concatenate(cols, axis=1)


def _prompt_in_kernel(x_ref, mod_ref, w_ref, cw_ref, tab_ref,
                      yconv_ref, q_ref, kb_ref, vb_ref, k_ref, v_ref, cst_ref, carry_ref, ubuf_ref, *, ts, cc):
    s, b = pl.program_id(0), pl.program_id(1)

    @pl.when(s == 0)
    def _():
        carry_ref[b] = jnp.zeros((SUBLANES, cc), F32)

    mod = mod_ref[0]
    xm = (x_ref[0] * (1.0 + mod[1:2]) + mod[0:1]).astype(BF16)

    z = jnp.dot(xm, w_ref[:, 0:3 * cc], preferred_element_type=F32)
    h, gb, gc = z[:, 0:cc], z[:, cc:2 * cc], z[:, 2 * cc:3 * cc]
    u = gc * h

    ubuf_ref[0:SUBLANES] = carry_ref[b]
    ubuf_ref[SUBLANES:SUBLANES + ts] = u
    u1 = ubuf_ref[SUBLANES - 1:SUBLANES - 1 + ts]
    u2 = ubuf_ref[SUBLANES - 2:SUBLANES - 2 + ts]
    cw = cw_ref[...]
    yconv_ref[0] = (gb * (cw[0:1] * u2 + cw[1:2] * u1 + cw[2:3] * u)).astype(BF16)
    carry_ref[b] = u[ts - SUBLANES:]
    cst_ref[0, 0] = u[ts - (CONV_K - 1):]

    z = jnp.dot(xm, w_ref[:, 3 * cc:], preferred_element_type=F32)
    hw = N_HEADS * HEAD_W
    q_ref[0] = (_rope(z[:, 0:hw], tab_ref) * Q_SCALE).astype(BF16)
    k = _rope(z[:, hw:2 * hw], tab_ref)
    v = z[:, 2 * hw:]
    kb_ref[0] = k.astype(BF16)
    vb_ref[0] = v.astype(BF16)
    for h in range(N_HEADS):
        rows = pl.ds(h, ts, stride=N_HEADS)
        k_ref[0, rows, :] = k[:, h * HEAD_W:(h + 1) * HEAD_W]
        v_ref[0, rows, :] = v[:, h * V_HEAD:(h + 1) * V_HEAD]


def _prompt_in(x, mod, w_in, conv_w, tab, ts):
    bsz, seq, d = x.shape
    cc = conv_w.shape[1]
    hw = N_HEADS * HEAD_W
    n_in = w_in.shape[1]
    tok = lambda s, b: (b, s, 0)
    kern = functools.partial(_prompt_in_kernel, ts=ts, cc=cc)
    return pl.pallas_call(
        kern,
        grid=(seq // ts, bsz),
        in_specs=[pl.BlockSpec((1, ts, d), tok),
                  pl.BlockSpec((1, N_MOD, d), lambda s, b: (b, 0, 0)),
                  _resident((d, n_in)),
                  _resident((CONV_K, cc)),
                  pl.BlockSpec((3, ts, LANES), lambda s, b: (0, s, 0))],
        out_specs=[pl.BlockSpec((1, ts, cc), tok),
                   pl.BlockSpec((1, ts, hw), tok),
                   pl.BlockSpec((1, ts, hw), tok),
                   pl.BlockSpec((1, ts, N_HEADS * V_HEAD), tok),
                   pl.BlockSpec((1, ts * N_HEADS, HEAD_W), tok),
                   pl.BlockSpec((1, ts * N_HEADS, V_HEAD), tok),
                   pl.BlockSpec((1, 1, CONV_K - 1, cc), lambda s, b: (s, b, 0, 0))],
        out_shape=[jax.ShapeDtypeStruct((bsz, seq, cc), BF16),
                   jax.ShapeDtypeStruct((bsz, seq, hw), BF16),
                   jax.ShapeDtypeStruct((bsz, seq, hw), BF16),
                   jax.ShapeDtypeStruct((bsz, seq, N_HEADS * V_HEAD), BF16),
                   jax.ShapeDtypeStruct((bsz, seq * N_HEADS, HEAD_W), F32),
                   jax.ShapeDtypeStruct((bsz, seq * N_HEADS, V_HEAD), F32),
                   jax.ShapeDtypeStruct((seq // ts, bsz, CONV_K - 1, cc), F32)],
        scratch_shapes=[pltpu.VMEM((bsz, SUBLANES, cc), F32),
                        pltpu.VMEM((ts + SUBLANES, cc), F32)],
        compiler_params=_params(2),
        name="prompt_in",
    )(x, mod, w_in, conv_w, tab)


def _lambda_value(lam_ref, lam_init):
    lp = lam_ref[...]
    a = jnp.sum(lp[0:1] * lp[1:2], axis=1, keepdims=True)
    b = jnp.sum(lp[2:3] * lp[3:4], axis=1, keepdims=True)
    return jnp.exp(a) - jnp.exp(b) + lam_init


def _head_out(o1, o2, lam, g, lam_init):
    d = o1 - lam * o2
    d = d * lax.rsqrt(jnp.mean(d * d, axis=-1, keepdims=True) + LN_EPS) * g
    return d * (1.0 - lam_init)


def _softmax_step(carry, s, v):
    m, l, acc = carry
    m_new = jnp.maximum(m, jnp.max(s, axis=1, keepdims=True))
    alpha = jnp.exp2(m - m_new)
    p = jnp.exp2(s - m_new)
    l = alpha * l + jnp.sum(p, axis=1, keepdims=True)
    acc = alpha * acc + jnp.dot(p.astype(BF16), v, preferred_element_type=F32)
    return m_new, l, acc


def _prompt_attn_kernel(q_ref, k_ref, v_ref, lam_ref, g_ref, o_ref, *, seq, tq, lam_init):
    lam = _lambda_value(lam_ref, lam_init)
    lane = lax.broadcasted_iota(jnp.int32, (tq, HEAD_W), 1)
    nt = (((1,), (1,)), ((), ()))

    def stacked_queries(qi):
        q = q_ref[0, qi * tq:(qi + 1) * tq, :]
        zero = jnp.zeros_like(q)
        return jnp.concatenate([jnp.where(lane < QK_HEAD, q, zero), jnp.where(lane >= QK_HEAD, q, zero)], axis=0)

    def scores(qq, qi, ki):
        s = lax.dot_general(qq, k_ref[0, ki * tq:(ki + 1) * tq, :], nt, preferred_element_type=F32)
        if ki == qi:
            row = lax.broadcasted_iota(jnp.int32, s.shape, 0)
            col = lax.broadcasted_iota(jnp.int32, s.shape, 1)
            s = jnp.where(col <= jnp.where(row >= tq, row - tq, row), s, NEG_INF)
        return s

    def finish(qi, carry):
        _, l, acc = carry
        o = acc / l
        y = _head_out(o[:tq], o[tq:], lam, g_ref[...], lam_init)
        o_ref[0, qi * tq:(qi + 1) * tq, :] = y.astype(BF16)

    for qi in range(seq // tq):
        qq = stacked_queries(qi)
        carry = (jnp.full((2 * tq, 1), NEG_INF, F32), jnp.zeros((2 * tq, 1), F32), jnp.zeros((2 * tq, V_HEAD), F32))
        for ki in range(qi + 1):
            carry = _softmax_step(carry, scores(qq, qi, ki), v_ref[0, ki * tq:(ki + 1) * tq, :])
        finish(qi, carry)


def _prompt_attn(q, k, v, lam_params, g, lam_init, tq):
    bsz, seq, _ = q.shape
    blk = pl.BlockSpec((1, seq, HEAD_W), lambda b, h: (b, 0, h))
    kern = functools.partial(_prompt_attn_kernel, seq=seq, tq=tq, lam_init=lam_init)
    return pl.pallas_call(
        kern,
        grid=(bsz, N_HEADS),
        in_specs=[blk, blk, blk,
                  pl.BlockSpec((4, QK_HEAD), lambda b, h: (0, 0)),
                  pl.BlockSpec((1, V_HEAD), lambda b, h: (0, 0))],
        out_specs=blk,
        out_shape=jax.ShapeDtypeStruct((bsz, seq, N_HEADS * V_HEAD), BF16),
        compiler_params=_params(2),
        name="prompt_attn",
    )(q, k, v, lam_params, g)


def _finish_math(x, yc, ya, gate1, shift2, scale2, gate2, wo_ref, ln1g, ln1b, wg_ref, wu_ref, wd_ref,
                 ln2g, ln2b, alpha):
    cc = yc.shape[1]
    y = (jnp.dot(yc, wo_ref[0:cc], preferred_element_type=F32)
         + jnp.dot(ya, wo_ref[cc:], preferred_element_type=F32))
    x1 = _layer_norm(alpha * x + gate1 * y, ln1g, ln1b)
    hf = (x1 * (1.0 + scale2) + shift2).astype(BF16)
    g = jnp.dot(hf, wg_ref[...], preferred_element_type=F32)
    u = jnp.dot(hf, wu_ref[...], preferred_element_type=F32)
    act = (jax.nn.silu(g) * u).astype(BF16)
    f = jnp.dot(act, wd_ref[...], preferred_element_type=F32)
    return _layer_norm(alpha * x1 + gate2 * f, ln2g, ln2b)


def _prompt_finish_kernel(x_ref, yc_ref, ya_ref, mod_ref, wo_ref, ln1_ref, wg_ref, wu_ref, wd_ref, ln2_ref,
                          o_ref, *, alpha):
    mod = mod_ref[0]
    o_ref[0] = _finish_math(x_ref[0], yc_ref[0], ya_ref[0], mod[2:3], mod[3:4], mod[4:5], mod[5:6],
                            wo_ref, ln1_ref[0:1], ln1_ref[1:2], wg_ref, wu_ref, wd_ref,
                            ln2_ref[0:1], ln2_ref[1:2], alpha)


def _sample_finish_kernel(x_ref, yc_ref, o1_ref, o2_ref, lam_ref, g_ref, mod_ref, wo_ref, ln1_ref,
                          wg_ref, wu_ref, wd_ref, ln2_ref, o_ref, *, alpha, lam_init):
    lam = _lambda_value(lam_ref, lam_init)
    o1, o2 = o1_ref[...], o2_ref[...]
    ya = jnp.concatenate(
        [_head_out(o1[:, h * V_HEAD:(h + 1) * V_HEAD], o2[:, h * V_HEAD:(h + 1) * V_HEAD], lam, g_ref[...], lam_init)
         for h in range(N_HEADS)], axis=1).astype(BF16)
    o_ref[...] = _finish_math(x_ref[...], yc_ref[...], ya, mod_ref[2], mod_ref[3], mod_ref[4], mod_ref[5],
                              wo_ref, ln1_ref[0:1], ln1_ref[1:2], wg_ref, wu_ref, wd_ref,
                              ln2_ref[0:1], ln2_ref[1:2], alpha)


def _prompt_finish(x, yc, ya, mod, w_out, ln1, w_gate, w_up, w_down, ln2, alpha, tm):
    bsz, seq, d = x.shape
    cc, d_ff = yc.shape[2], w_gate.shape[1]
    tok = lambda b, s: (b, s, 0)
    kern = functools.partial(_prompt_finish_kernel, alpha=alpha)
    return pl.pallas_call(
        kern,
        grid=(bsz, seq // tm),
        in_specs=[pl.BlockSpec((1, tm, d), tok),
                  pl.BlockSpec((1, tm, cc), tok),
                  pl.BlockSpec((1, tm, d - cc), tok),
                  pl.BlockSpec((1, N_MOD, d), lambda b, s: (b, 0, 0)),
                  _resident((d, d)), _resident((2, d)),
                  _resident((d, d_ff)), _resident((d, d_ff)), _resident((d_ff, d)), _resident((2, d))],
        out_specs=pl.BlockSpec((1, tm, d), tok),
        out_shape=jax.ShapeDtypeStruct((bsz, seq, d), F32),
        compiler_params=_params(2),
        name="prompt_finish",
    )(x, yc, ya, mod, w_out, ln1, w_gate, w_up, w_down, ln2)


def _sample_finish(x, yc, o1, o2, lam_params, g, mod_rows, w_out, ln1, w_gate, w_up, w_down, ln2, alpha, lam_init):
    n, d = x.shape
    kern = functools.partial(_sample_finish_kernel, alpha=alpha, lam_init=lam_init)
    args = (x, yc, o1, o2, lam_params, g, mod_rows, w_out, ln1, w_gate, w_up, w_down, ln2)
    return pl.pallas_call(
        kern,
        grid=(1,),
        in_specs=[_resident(a.shape) for a in args],
        out_specs=pl.BlockSpec((n, d), lambda i: (0, 0)),
        out_shape=jax.ShapeDtypeStruct((n, d), F32),
        compiler_params=_params(1),
        name="sample_finish",
    )(*args)


def _sample_in_kernel(x_ref, mod_ref, w_ref, cw_ref, st1_ref, st2_ref, tab_ref,
                      yconv_ref, q_ref, k_ref, v_ref, u_ref, *, t_len, cc):
    xm = (x_ref[...] * (1.0 + mod_ref[1]) + mod_ref[0]).astype(BF16)
    z = jnp.dot(xm, w_ref[...], preferred_element_type=F32)
    h, gb, gc = z[:, 0:cc], z[:, cc:2 * cc], z[:, 2 * cc:3 * cc]
    u = gc * h
    n = u.shape[0]
    t = lax.broadcasted_iota(jnp.int32, (n, cc), 0) % t_len
    u1 = jnp.where(t >= 1, pltpu.roll(u, 1, 0), st1_ref[...])
    u2 = jnp.where(t >= 2, pltpu.roll(u, 2, 0), st2_ref[...])
    cw = cw_ref[...]
    yconv_ref[...] = (gb * (cw[0:1] * u2 + cw[1:2] * u1 + cw[2:3] * u)).astype(BF16)
    u_ref[...] = u
    hw = N_HEADS * HEAD_W
    q_ref[...] = (_rope(z[:, 3 * cc:3 * cc + hw], tab_ref) * Q_SCALE).astype(BF16)
    k_ref[...] = _rope(z[:, 3 * cc + hw:3 * cc + 2 * hw], tab_ref)
    v_ref[...] = z[:, 3 * cc + 2 * hw:]


def _sample_in(x, mod_rows, w_in, conv_w, st1, st2, tab, t_len):
    n, _ = x.shape
    cc = conv_w.shape[1]
    hw = N_HEADS * HEAD_W
    kern = functools.partial(_sample_in_kernel, t_len=t_len, cc=cc)
    args = (x, mod_rows, w_in, conv_w, st1, st2, tab)
    out_shapes = [jax.ShapeDtypeStruct((n, cc), BF16), jax.ShapeDtypeStruct((n, hw), BF16),
                  jax.ShapeDtypeStruct((n, hw), F32), jax.ShapeDtypeStruct((n, N_HEADS * V_HEAD), F32),
                  jax.ShapeDtypeStruct((n, cc), F32)]
    return pl.pallas_call(
        kern,
        grid=(1,),
        in_specs=[_resident(a.shape) for a in args],
        out_specs=[pl.BlockSpec(o.shape, lambda i: (0, 0)) for o in out_shapes],
        out_shape=out_shapes,
        compiler_params=_params(1),
        name="sample_in",
    )(*args)


def _sample_attn_kernel(pt_ref, wq_ref, kn_ref, vn_ref, ck_ref, cv_ref, o_ref, kbuf, vbuf, sem,
                        *, n_pages, pages_per_chunk, page_base, t_len):
    b = pl.program_id(0)
    n_chunks = n_pages // pages_per_chunk
    total_chunks = pl.num_programs(0) * n_chunks
    n_slots = kbuf.shape[0]
    lookahead = n_slots - 1
    page_rows = ck_ref.shape[1]
    nt = (((1,), (1,)), ((), ()))

    def copies(index):
        batch, chunk, slot = index // n_chunks, index % n_chunks, index % n_slots
        out = []
        for g in range(pages_per_chunk):
            pg = pt_ref[batch, chunk * pages_per_chunk + g] + page_base
            dst = pl.ds(g * page_rows, page_rows)
            out.append((pltpu.make_async_copy(ck_ref.at[pg], kbuf.at[slot, dst], sem.at[0, slot]),
                        pltpu.make_async_copy(cv_ref.at[pg], vbuf.at[slot, dst], sem.at[1, slot])))
        return out

    def start(index):
        for k_copy, v_copy in copies(index):
            k_copy.start(priority=0)
            v_copy.start(priority=1)

    @pl.when(b == 0)
    def _():
        for index in range(lookahead):
            start(index)

    wq = wq_ref[0]
    rows = wq.shape[0]
    row_head = lax.broadcasted_iota(jnp.int32, (rows, page_rows), 0) // (2 * t_len)
    col_head = lax.broadcasted_iota(jnp.int32, (rows, page_rows), 1) % N_HEADS
    own_head = row_head == col_head
    head_bias = jnp.where(own_head, 0.0, NEG_INF)
    chunk_bias = jnp.concatenate([head_bias] * pages_per_chunk, axis=1)

    def chunk_step(c, carry):
        index = b * n_chunks + c
        slot = index % n_slots

        @pl.when(index + lookahead < total_chunks)
        def _():
            start(index + lookahead)

        for k_copy, v_copy in copies(index):
            k_copy.wait()
            v_copy.wait()
        s = lax.dot_general(wq, kbuf[slot].astype(BF16), nt, preferred_element_type=F32) + chunk_bias
        return _softmax_step(carry, s, vbuf[slot].astype(BF16))

    carry = (jnp.full((rows, 1), NEG_INF, F32), jnp.zeros((rows, 1), F32), jnp.zeros((rows, V_HEAD), F32))
    carry = lax.fori_loop(0, n_chunks, chunk_step, carry)

    s = lax.dot_general(wq, kn_ref[0].astype(BF16), nt, preferred_element_type=F32)
    new_rows = kn_ref.shape[1]
    q_step = lax.broadcasted_iota(jnp.int32, (rows, new_rows), 0) % t_len
    k_step = lax.broadcasted_iota(jnp.int32, (rows, new_rows), 1) // N_HEADS
    s = jnp.where(own_head[:, :new_rows] & (k_step <= q_step), s, NEG_INF)
    _, l, acc = _softmax_step(carry, s, vn_ref[0].astype(BF16))
    o_ref[0] = acc / l


def _sample_attn(page_table, wq, k_new, v_new, cache_k, cache_v, page_base, t_len):
    bsz, n_pages = page_table.shape
    _, page_rows, width = cache_k.shape
    rows = wq.shape[1]
    g = _largest_divisor(n_pages, 8)
    n_slots = min(4, bsz * (n_pages // g))
    kern = functools.partial(_sample_attn_kernel, n_pages=n_pages, pages_per_chunk=g,
                             page_base=page_base, t_len=t_len)
    per_batch = lambda r: pl.BlockSpec((1, r, width), lambda b, pt: (b, 0, 0))
    grid_spec = pltpu.PrefetchScalarGridSpec(
        num_scalar_prefetch=1,
        grid=(bsz,),
        in_specs=[per_batch(rows), per_batch(k_new.shape[1]), per_batch(v_new.shape[1]),
                  pl.BlockSpec(memory_space=pl.ANY),
                  pl.BlockSpec(memory_space=pl.ANY)],
        out_specs=per_batch(rows),
        scratch_shapes=[pltpu.VMEM((n_slots, g * page_rows, width), F32),
                        pltpu.VMEM((n_slots, g * page_rows, width), F32),
                        pltpu.SemaphoreType.DMA((2, n_slots))],
    )
    return pl.pallas_call(
        kern,
        grid_spec=grid_spec,
        out_shape=jax.ShapeDtypeStruct((bsz, rows, width), F32),
        compiler_params=_params(1),
        name="sample_attn",
    )(page_table, wq, k_new, v_new, cache_k, cache_v)


def _query_rows(q, bsz, t_len):
    q5 = q.reshape(bsz, t_len, N_HEADS, 2, QK_HEAD).transpose(0, 2, 3, 1, 4)
    eye_c = jnp.eye(2, dtype=q.dtype)
    w = q5[:, :, :, :, None, :] * eye_c[None, None, :, None, :, None]
    return w.reshape(bsz, N_HEADS * 2 * t_len, HEAD_W)


def _token_head_rows(a, bsz, t_len, rows):
    a = a.reshape(bsz, t_len * N_HEADS, a.shape[1] // N_HEADS)
    return jnp.pad(a, ((0, 0), (0, rows - t_len * N_HEADS), (0, 0)))


def _component_outputs(o, bsz, t_len):
    o5 = o.reshape(bsz, N_HEADS, 2, t_len, V_HEAD).transpose(2, 0, 3, 1, 4)
    o5 = o5.reshape(2, bsz * t_len, N_HEADS * V_HEAD)
    return o5[0], o5[1]


def kernel(x_prompt, x_sample, c_prompt, c_sample, cache_k, cache_v, state_conv, page_table, w_ada, b_ada, w_in, conv_w, lambda_q1, lambda_k1, lambda_q2, lambda_k2, subln_g, w_out, ln1_g, ln1_b, w_gate, w_up, w_down, ln2_g, ln2_b):
    bp, s_len, d = x_prompt.shape
    bs, t_len, _ = x_sample.shape
    depth, n_pool, page = cache_k.shape[:3]
    past = page_table.shape[1] * page
    cc = conv_w.shape[2]
    alpha = (2.0 * depth) ** 0.25
    assert (bs * t_len) % SUBLANES == 0 and t_len <= page

    ts = min(1024, s_len)
    tq = min(512, s_len)
    tm = min(512, s_len)

    tab_p = _rope_tables(jnp.arange(s_len))
    tab_s = jnp.tile(_rope_tables(past + jnp.arange(t_len)), (1, bs, 1))
    ck = cache_k.reshape(depth * n_pool, page * N_HEADS, HEAD_W)
    cv = cache_v.reshape(depth * n_pool, page * N_HEADS, V_HEAD)
    c_all = jnp.concatenate([c_prompt, c_sample], axis=0)

    xp, xs = x_prompt, x_sample.reshape(bs * t_len, d)
    kp_l, vp_l, cp_l, ks_l, vs_l, cs_l = [], [], [], [], [], []
    for l in range(depth):
        lam_init = 0.8 - 0.6 * math.exp(-0.3 * l)
        w_in_b, w_out_b = w_in[l].astype(BF16), w_out[l].astype(BF16)
        w_gate_b, w_up_b, w_down_b = w_gate[l].astype(BF16), w_up[l].astype(BF16), w_down[l].astype(BF16)
        lam_params = jnp.stack([lambda_q1[l], lambda_k1[l], lambda_q2[l], lambda_k2[l]])
        g = subln_g[l].reshape(1, V_HEAD)
        ln1 = jnp.stack([ln1_g[l], ln1_b[l]])
        ln2 = jnp.stack([ln2_g[l], ln2_b[l]])

        mod = _adaln(c_all, w_ada[l], b_ada[l]).reshape(bp + bs, N_MOD, d)
        mod_p = mod[:bp]
        mod_s = jnp.repeat(mod[bp:].transpose(1, 0, 2), t_len, axis=1)

        yconv, q, kb, vb, k, v, cst = _prompt_in(xp, mod_p, w_in_b, conv_w[l], tab_p, ts)
        yattn = _prompt_attn(q, kb, vb, lam_params, g, lam_init, tq)
        xp = _prompt_finish(xp, yconv, yattn, mod_p, w_out_b, ln1, w_gate_b, w_up_b, w_down_b, ln2, alpha, tm)
        kp_l.append(k.reshape(bp, s_len, N_HEADS, HEAD_W))
        vp_l.append(v.reshape(bp, s_len, N_HEADS, V_HEAD))
        cp_l.append(cst[-1])

        st = state_conv[l]
        zero = jnp.zeros((bs, 1, cc), F32)
        st1 = jnp.concatenate([st[:, 1:2]] + [zero] * (t_len - 1), axis=1).reshape(bs * t_len, cc)
        st2 = jnp.concatenate([st[:, 0:1], st[:, 1:2]] + [zero] * (t_len - 2), axis=1).reshape(bs * t_len, cc)
        yconv_s, q_s, k_s, v_s, u_s = _sample_in(xs, mod_s, w_in_b, conv_w[l], st1, st2, tab_s, t_len)
        wq = _query_rows(q_s, bs, t_len)
        rows = pl.cdiv(t_len * N_HEADS, LANES) * LANES
        o = _sample_attn(page_table, wq, _token_head_rows(k_s, bs, t_len, rows),
                         _token_head_rows(v_s, bs, t_len, rows), ck, cv, l * n_pool, t_len)
        o1, o2 = _component_outputs(o, bs, t_len)
        xs = _sample_finish(xs, yconv_s, o1, o2, lam_params, g, mod_s, w_out_b, ln1, w_gate_b, w_up_b, w_down_b,
                            ln2, alpha, lam_init)
        ks_l.append(k_s.reshape(bs, t_len, N_HEADS, HEAD_W))
        vs_l.append(v_s.reshape(bs, t_len, N_HEADS, V_HEAD))
        cs_l.append(u_s.reshape(bs, t_len, cc)[:, t_len - (CONV_K - 1):])

    return (xp, xs.reshape(bs, t_len, d), jnp.stack(kp_l), jnp.stack(vp_l), jnp.stack(cp_l),
            jnp.stack(ks_l), jnp.stack(vs_l), jnp.stack(cs_l))
```

```python
import functools
import math

import jax
import jax.numpy as jnp
from jax import lax
from jax.experimental import pallas as pl
from jax.experimental.pallas import tpu as pltpu

F32 = jnp.float32
BF16 = jnp.bfloat16

N_HEADS = 4
QK_HEAD = 64
V_HEAD = 128
HEAD_W = 2 * QK_HEAD
CONV_K = 3
ROT_DIM = QK_HEAD // 4
ROPE_THETA = 500000.0
N_MOD = 6
LN_EPS = 1e-5
LANES = 128
SUBLANES = 8
SUM_ROWS = 16
VMEM_LIMIT_BYTES = 56 * 1024 * 1024
NEG_INF = float("-inf")
Q_SCALE = QK_HEAD ** -0.5 * math.log2(math.e)


def _params(n_axes):
    return pltpu.CompilerParams(dimension_semantics=("arbitrary",) * n_axes,
                                vmem_limit_bytes=VMEM_LIMIT_BYTES)


def _resident(shape):
    return pl.BlockSpec(shape, lambda *_: (0,) * len(shape), pipeline_mode=pl.Buffered(1))


def _largest_divisor(n, cap):
    return max(d for d in range(1, min(n, cap) + 1) if n % d == 0)


def _layer_norm(x, g, b):
    mu = jnp.mean(x, axis=-1, keepdims=True)
    xc = x - mu
    var = jnp.mean(xc * xc, axis=-1, keepdims=True)
    return xc * lax.rsqrt(var + LN_EPS) * g + b


def _adaln_kernel(c_ref, w_ref, b_ref, o_ref):
    a = jax.nn.silu(c_ref[...]).astype(BF16)
    o_ref[...] = jnp.dot(a, w_ref[...].astype(BF16), preferred_element_type=F32) + b_ref[...]


def _adaln(c, w, b):
    n, d = c.shape
    n_out = w.shape[1]
    tn = 768 if n_out % 768 == 0 else n_out
    return pl.pallas_call(
        _adaln_kernel,
        grid=(n_out // tn,),
        in_specs=[pl.BlockSpec((n, d), lambda j: (0, 0)),
                  pl.BlockSpec((d, tn), lambda j: (0, j)),
                  pl.BlockSpec((1, tn), lambda j: (0, j))],
        out_specs=pl.BlockSpec((n, tn), lambda j: (0, j)),
        out_shape=jax.ShapeDtypeStruct((n, n_out), F32),
        compiler_params=_params(1),
        name="adaln",
    )(c, w, b.reshape(1, n_out))


def _rope_tables(pos):
    inv = ROPE_THETA ** (-jnp.arange(0, ROT_DIM, 2, dtype=F32) / ROT_DIM)
    ang = pos.astype(F32)[:, None] * inv[None, :]
    cos, sin = jnp.cos(ang), jnp.sin(ang)
    t, half = pos.shape[0], ROT_DIM // 2
    ones = jnp.ones((t, QK_HEAD - ROT_DIM), F32)
    c = jnp.concatenate([cos, cos, ones], axis=1)
    s_up = jnp.concatenate([-sin, jnp.zeros((t, QK_HEAD - half), F32)], axis=1)
    s_dn = jnp.concatenate([jnp.zeros((t, half), F32), sin, jnp.zeros((t, QK_HEAD - ROT_DIM), F32)], axis=1)
    return jnp.stack([jnp.tile(a, (1, LANES // QK_HEAD)) for a in (c, s_up, s_dn)])


def _rope(x, tab_ref):
    c, s_up, s_dn = tab_ref[0], tab_ref[1], tab_ref[2]
    half = ROT_DIM // 2
    cols = []
    for i in range(x.shape[1] // LANES):
        xc = x[:, i * LANES:(i + 1) * LANES]
        cols.append(xc * c + pltpu.roll(xc, LANES - half, 1) * s_up + pltpu.roll(xc, half, 1) * s_dn)
    return jnp.concatenate(cols, axis=1)


def _prompt_in_kernel(x_ref, mod_ref, w_ref, cw_ref, tab_ref,
                      yconv_ref, q_ref, kb_ref, vt_ref, k_ref, v_ref, cst_ref, carry_ref, ubuf_ref, *, ts, cc):
    s, b = pl.program_id(0), pl.program_id(1)

    @pl.when(s == 0)
    def _():
        carry_ref[b] = jnp.zeros((SUBLANES, cc), F32)

    mod = mod_ref[0]
    xm = (x_ref[0] * (1.0 + mod[1:2]) + mod[0:1]).astype(BF16)

    z = jnp.dot(xm, w_ref[:, 0:3 * cc], preferred_element_type=F32)
    h, gb, gc = z[:, 0:cc], z[:, cc:2 * cc], z[:, 2 * cc:3 * cc]
    u = gc * h

    ubuf_ref[0:SUBLANES] = carry_ref[b]
    ubuf_ref[SUBLANES:SUBLANES + ts] = u
    u1 = ubuf_ref[SUBLANES - 1:SUBLANES - 1 + ts]
    u2 = ubuf_ref[SUBLANES - 2:SUBLANES - 2 + ts]
    cw = cw_ref[...]
    yconv_ref[0] = (gb * (cw[0:1] * u2 + cw[1:2] * u1 + cw[2:3] * u)).astype(BF16)
    carry_ref[b] = u[ts - SUBLANES:]
    cst_ref[0, 0] = u[ts - (CONV_K - 1):]

    z = jnp.dot(xm, w_ref[:, 3 * cc:], preferred_element_type=F32)
    hw = N_HEADS * HEAD_W
    q_ref[0] = (_rope(z[:, 0:hw], tab_ref) * Q_SCALE).astype(BF16)
    k = _rope(z[:, hw:2 * hw], tab_ref)
    v = z[:, 2 * hw:]
    kb_ref[0] = k.astype(BF16)
    vt_ref[0] = v.T.astype(BF16)
    for h in range(N_HEADS):
        rows = pl.ds(h, ts, stride=N_HEADS)
        k_ref[0, rows, :] = k[:, h * HEAD_W:(h + 1) * HEAD_W]
        v_ref[0, rows, :] = v[:, h * V_HEAD:(h + 1) * V_HEAD]


def _prompt_in(x, mod, w_in, conv_w, tab, ts):
    bsz, seq, d = x.shape
    cc = conv_w.shape[1]
    hw = N_HEADS * HEAD_W
    n_in = w_in.shape[1]
    tok = lambda s, b: (b, s, 0)
    kern = functools.partial(_prompt_in_kernel, ts=ts, cc=cc)
    return pl.pallas_call(
        kern,
        grid=(seq // ts, bsz),
        in_specs=[pl.BlockSpec((1, ts, d), tok),
                  pl.BlockSpec((1, N_MOD, d), lambda s, b: (b, 0, 0)),
                  _resident((d, n_in)),
                  _resident((CONV_K, cc)),
                  pl.BlockSpec((3, ts, LANES), lambda s, b: (0, s, 0))],
        out_specs=[pl.BlockSpec((1, ts, cc), tok),
                   pl.BlockSpec((1, ts, hw), tok),
                   pl.BlockSpec((1, ts, hw), tok),
                   pl.BlockSpec((1, N_HEADS * V_HEAD, ts), lambda s, b: (b, 0, s)),
                   pl.BlockSpec((1, ts * N_HEADS, HEAD_W), tok),
                   pl.BlockSpec((1, ts * N_HEADS, V_HEAD), tok),
                   pl.BlockSpec((1, 1, CONV_K - 1, cc), lambda s, b: (s, b, 0, 0))],
        out_shape=[jax.ShapeDtypeStruct((bsz, seq, cc), BF16),
                   jax.ShapeDtypeStruct((bsz, seq, hw), BF16),
                   jax.ShapeDtypeStruct((bsz, seq, hw), BF16),
                   jax.ShapeDtypeStruct((bsz, N_HEADS * V_HEAD, seq), BF16),
                   jax.ShapeDtypeStruct((bsz, seq * N_HEADS, HEAD_W), F32),
                   jax.ShapeDtypeStruct((bsz, seq * N_HEADS, V_HEAD), F32),
                   jax.ShapeDtypeStruct((seq // ts, bsz, CONV_K - 1, cc), F32)],
        scratch_shapes=[pltpu.VMEM((bsz, SUBLANES, cc), F32),
                        pltpu.VMEM((ts + SUBLANES, cc), F32)],
        compiler_params=_params(2),
        name="prompt_in",
    )(x, mod, w_in, conv_w, tab)


def _lambda_value(lam_ref, lam_init):
    lp = lam_ref[...]
    a = jnp.sum(lp[0:1] * lp[1:2], axis=1, keepdims=True)
    b = jnp.sum(lp[2:3] * lp[3:4], axis=1, keepdims=True)
    return jnp.exp(a) - jnp.exp(b) + lam_init


def _head_out(o1, o2, lam, g, lam_init):
    d = o1 - lam * o2
    d = d * lax.rsqrt(jnp.mean(d * d, axis=-1, keepdims=True) + LN_EPS) * g
    return d * (1.0 - lam_init)


def _softmax_step(carry, s, v):
    m, l, acc = carry
    m_new = jnp.maximum(m, jnp.max(s, axis=1, keepdims=True))
    alpha = jnp.exp2(m - m_new)
    p = jnp.exp2(s - m_new)
    l = alpha * l + jnp.sum(p, axis=1, keepdims=True)
    acc = alpha * acc + jnp.dot(p.astype(BF16), v, preferred_element_type=F32)
    return m_new, l, acc


def _prompt_attn_kernel(q_ref, k_ref, vt_ref, lam_ref, g_ref, o_ref, *, seq, tq, lam_init):
    lam = _lambda_value(lam_ref, lam_init)
    lane = lax.broadcasted_iota(jnp.int32, (tq, HEAD_W), 1)
    nt = (((1,), (1,)), ((), ()))
    g_col = g_ref[...]

    def stacked_queries(qi):
        q = q_ref[0, qi * tq:(qi + 1) * tq, :]
        zero = jnp.zeros_like(q)
        return jnp.concatenate([jnp.where(lane < QK_HEAD, q, zero), jnp.where(lane >= QK_HEAD, q, zero)], axis=0)

    ones_rows = jnp.ones((SUM_ROWS, tq), BF16)

    def step(carry, qq, qi, ki):
        m, acc = carry
        keys = slice(ki * tq, (ki + 1) * tq)
        s = lax.dot_general(k_ref[0, keys, :], qq, nt, preferred_element_type=F32)
        if ki == qi:
            key = lax.broadcasted_iota(jnp.int32, s.shape, 0)
            col = lax.broadcasted_iota(jnp.int32, s.shape, 1)
            s = jnp.where(key <= jnp.where(col >= tq, col - tq, col), s, NEG_INF)
        m_new = jnp.maximum(m, jnp.max(s, axis=0, keepdims=True))
        p = jnp.exp2(s - m_new).astype(BF16)
        vt_ones = jnp.concatenate([vt_ref[0, :, keys], ones_rows], axis=0)
        acc = jnp.exp2(m - m_new) * acc + jnp.dot(vt_ones, p, preferred_element_type=F32)
        return m_new, acc

    for qi in range(seq // tq):
        qq = stacked_queries(qi)
        carry = (jnp.full((1, 2 * tq), NEG_INF, F32), jnp.zeros((V_HEAD + SUM_ROWS, 2 * tq), F32))
        for ki in range(qi + 1):
            carry = step(carry, qq, qi, ki)
        _, acc = carry
        o = acc[:V_HEAD] / acc[V_HEAD:V_HEAD + 1]
        d = o[:, :tq] - lam * o[:, tq:]
        d = d * lax.rsqrt(jnp.mean(d * d, axis=0, keepdims=True) + LN_EPS) * g_col * (1.0 - lam_init)
        o_ref[0, qi * tq:(qi + 1) * tq, :] = d.T.astype(BF16)


def _prompt_attn(q, k, vt, lam_params, g, lam_init, tq):
    bsz, seq, _ = q.shape
    blk = pl.BlockSpec((1, seq, HEAD_W), lambda b, h: (b, 0, h))
    kern = functools.partial(_prompt_attn_kernel, seq=seq, tq=tq, lam_init=lam_init)
    return pl.pallas_call(
        kern,
        grid=(bsz, N_HEADS),
        in_specs=[blk, blk,
                  pl.BlockSpec((1, V_HEAD, seq), lambda b, h: (b, h, 0)),
                  pl.BlockSpec((4, QK_HEAD), lambda b, h: (0, 0)),
                  pl.BlockSpec((V_HEAD, 1), lambda b, h: (0, 0))],
        out_specs=blk,
        out_shape=jax.ShapeDtypeStruct((bsz, seq, N_HEADS * V_HEAD), BF16),
        compiler_params=_params(2),
        name="prompt_attn",
    )(q, k, vt, lam_params, g.reshape(V_HEAD, 1))


FINISH_PHASES = 4
SAMPLE_CHUNK_PAGES = 16
SAMPLE_RING_SLOTS = 2


def _finish_math(x, yc, ya, gate1, shift2, scale2, gate2, wo_ref, ln1g, ln1b, wg_ref, wu_ref, wd_ref,
                 ln2g, ln2b, alpha, around_phase=None):
    run = around_phase or (lambda i, phase: phase())
    cc = yc.shape[1]
    v = {}

    def project():
        y = (jnp.dot(yc, wo_ref[0:cc], preferred_element_type=F32)
             + jnp.dot(ya, wo_ref[cc:], preferred_element_type=F32))
        v["x1"] = _layer_norm(alpha * x + gate1 * y, ln1g, ln1b)
        v["hf"] = (v["x1"] * (1.0 + scale2) + shift2).astype(BF16)

    def gate():
        v["g"] = jnp.dot(v["hf"], wg_ref[...], preferred_element_type=F32)

    def up():
        u = jnp.dot(v["hf"], wu_ref[...], preferred_element_type=F32)
        v["act"] = (jax.nn.silu(v["g"]) * u).astype(BF16)

    def down():
        f = jnp.dot(v["act"], wd_ref[...], preferred_element_type=F32)
        v["out"] = _layer_norm(alpha * v["x1"] + gate2 * f, ln2g, ln2b)

    for i, phase in enumerate((project, gate, up, down)):
        run(i, phase)
    return v["out"]


def _finish_attn_kernel(pt_ref, x_ref, yc_ref, ya_ref, mod_ref, wo_ref, ln1_ref, wg_ref, wu_ref, wd_ref, ln2_ref,
                        wq_ref, kn_ref, vn_ref, ck_ref, cv_ref, o_ref, so_ref,
                        kbuf, vbuf, sem, m_ref, l_ref, acc_ref,
                        *, alpha, n_pages, pages_per_chunk, chunks_per_step, page_base, t_len):
    t = pl.program_id(0) * pl.num_programs(1) + pl.program_id(1)
    n_chunks = n_pages // pages_per_chunk
    total_chunks = pl.num_programs(0) * pl.num_programs(1) * chunks_per_step
    n_slots = kbuf.shape[0]
    page_rows = ck_ref.shape[1]
    nt = (((1,), (1,)), ((), ()))
    rows = wq_ref.shape[1]

    def copies(index):
        batch, chunk, slot = index // n_chunks, index % n_chunks, index % n_slots
        out = []
        for g in range(pages_per_chunk):
            pg = pt_ref[batch, chunk * pages_per_chunk + g] + page_base
            dst = pl.ds(g * page_rows, page_rows)
            out.append((pltpu.make_async_copy(ck_ref.at[pg], kbuf.at[slot, dst], sem.at[0, slot]),
                        pltpu.make_async_copy(cv_ref.at[pg], vbuf.at[slot, dst], sem.at[1, slot])))
        return out

    def start(index):
        for k_copy, v_copy in copies(index):
            k_copy.start(priority=0)
            v_copy.start(priority=1)

    def reset_state():
        m_ref[...] = jnp.full(m_ref.shape, NEG_INF, F32)
        l_ref[...] = jnp.zeros(l_ref.shape, F32)
        acc_ref[...] = jnp.zeros(acc_ref.shape, F32)

    @pl.when(t == 0)
    def _():
        reset_state()
        for index in range(n_slots):
            start(index)

    row_head = lax.broadcasted_iota(jnp.int32, (rows, page_rows), 0) // (2 * t_len)
    col_head = lax.broadcasted_iota(jnp.int32, (rows, page_rows), 1) % N_HEADS
    own_head = row_head == col_head

    def attend(s, v):
        m, l, acc = _softmax_step((m_ref[...], l_ref[...], acc_ref[...]), s, v)
        m_ref[...] = m
        l_ref[...] = l
        acc_ref[...] = acc

    def consume(index):
        slot = index % n_slots
        for k_copy, v_copy in copies(index):
            k_copy.wait()
            v_copy.wait()
        head_bias = jnp.where(own_head, 0.0, NEG_INF)
        chunk_bias = jnp.concatenate([head_bias] * pages_per_chunk, axis=1)
        s = lax.dot_general(wq_ref[0], kbuf[slot].astype(BF16), nt, preferred_element_type=F32) + chunk_bias
        attend(s, vbuf[slot].astype(BF16))

    def refill(index):
        @pl.when(index + n_slots < total_chunks)
        def _():
            start(index + n_slots)

    def around_phase(i, matmul_phase):
        for k in range(chunks_per_step):
            if (0 if 2 * k < chunks_per_step else FINISH_PHASES // 2) == i:
                consume(t * chunks_per_step + k)
                refill(t * chunks_per_step + k)
        matmul_phase()

    mod = mod_ref[0]
    o_ref[0] = _finish_math(x_ref[0], yc_ref[0], ya_ref[0], mod[2:3], mod[3:4], mod[4:5], mod[5:6],
                            wo_ref, ln1_ref[0:1], ln1_ref[1:2], wg_ref, wu_ref, wd_ref,
                            ln2_ref[0:1], ln2_ref[1:2], alpha, around_phase)

    @pl.when(((t + 1) * chunks_per_step) % n_chunks == 0)
    def _():
        s = lax.dot_general(wq_ref[0], kn_ref[0].astype(BF16), nt, preferred_element_type=F32)
        new_rows = kn_ref.shape[1]
        q_step = lax.broadcasted_iota(jnp.int32, (rows, new_rows), 0) % t_len
        k_step = lax.broadcasted_iota(jnp.int32, (rows, new_rows), 1) // N_HEADS
        s = jnp.where(own_head[:, :new_rows] & (k_step <= q_step), s, NEG_INF)
        attend(s, vn_ref[0].astype(BF16))
        so_ref[0] = acc_ref[...] / l_ref[...]
        reset_state()


def _sample_finish_kernel(x_ref, yc_ref, o1_ref, o2_ref, lam_ref, g_ref, mod_ref, wo_ref, ln1_ref,
                          wg_ref, wu_ref, wd_ref, ln2_ref, o_ref, *, alpha, lam_init):
    lam = _lambda_value(lam_ref, lam_init)
    o1, o2 = o1_ref[...], o2_ref[...]
    ya = jnp.concatenate(
        [_head_out(o1[:, h * V_HEAD:(h + 1) * V_HEAD], o2[:, h * V_HEAD:(h + 1) * V_HEAD], lam, g_ref[...], lam_init)
         for h in range(N_HEADS)], axis=1).astype(BF16)
    o_ref[...] = _finish_math(x_ref[...], yc_ref[...], ya, mod_ref[2], mod_ref[3], mod_ref[4], mod_ref[5],
                              wo_ref, ln1_ref[0:1], ln1_ref[1:2], wg_ref, wu_ref, wd_ref,
                              ln2_ref[0:1], ln2_ref[1:2], alpha)


def _finish_attn(x, yc, ya, mod, w_out, ln1, w_gate, w_up, w_down, ln2, alpha, tm,
                 page_table, wq, k_new, v_new, cache_k, cache_v, page_base, t_len):
    bsz, seq, d = x.shape
    cc, d_ff = yc.shape[2], w_gate.shape[1]
    n_s = seq // tm
    sample_bsz, n_pages = page_table.shape
    _, page_rows, width = cache_k.shape
    rows = wq.shape[1]
    pages_per_step, ragged = divmod(sample_bsz * n_pages, bsz * n_s)
    assert not ragged and pages_per_step > 0
    g = _largest_divisor(math.gcd(pages_per_step, n_pages), SAMPLE_CHUNK_PAGES)
    chunks_per_step, n_chunks = pages_per_step // g, n_pages // g
    n_slots = SAMPLE_RING_SLOTS
    assert bsz * n_s * chunks_per_step >= n_slots
    kern = functools.partial(_finish_attn_kernel, alpha=alpha, n_pages=n_pages, pages_per_chunk=g,
                             chunks_per_step=chunks_per_step, page_base=page_base, t_len=t_len)
    tok = lambda b, s, pt: (b, s, 0)
    fixed = lambda shape: pl.BlockSpec(shape, lambda b, s, pt: (0,) * len(shape), pipeline_mode=pl.Buffered(1))
    sample = lambda r: pl.BlockSpec(
        (1, r, width), lambda b, s, pt: (((b * n_s + s) * chunks_per_step) // n_chunks, 0, 0))
    grid_spec = pltpu.PrefetchScalarGridSpec(
        num_scalar_prefetch=1,
        grid=(bsz, n_s),
        in_specs=[pl.BlockSpec((1, tm, d), tok),
                  pl.BlockSpec((1, tm, cc), tok),
                  pl.BlockSpec((1, tm, d - cc), tok),
                  pl.BlockSpec((1, N_MOD, d), lambda b, s, pt: (b, 0, 0)),
                  fixed((d, d)), fixed((2, d)),
                  fixed((d, d_ff)), fixed((d, d_ff)), fixed((d_ff, d)), fixed((2, d)),
                  sample(rows), sample(k_new.shape[1]), sample(v_new.shape[1]),
                  pl.BlockSpec(memory_space=pl.ANY),
                  pl.BlockSpec(memory_space=pl.ANY)],
        out_specs=[pl.BlockSpec((1, tm, d), tok), sample(rows)],
        scratch_shapes=[pltpu.VMEM((n_slots, g * page_rows, width), F32),
                        pltpu.VMEM((n_slots, g * page_rows, width), F32),
                        pltpu.SemaphoreType.DMA((2, n_slots)),
                        pltpu.VMEM((rows, 1), F32), pltpu.VMEM((rows, 1), F32), pltpu.VMEM((rows, V_HEAD), F32)],
    )
    return pl.pallas_call(
        kern,
        grid_spec=grid_spec,
        out_shape=[jax.ShapeDtypeStruct((bsz, seq, d), F32),
                   jax.ShapeDtypeStruct((sample_bsz, rows, width), F32)],
        compiler_params=_params(2),
        name="finish_attn",
    )(page_table, x, yc, ya, mod, w_out, ln1, w_gate, w_up, w_down, ln2, wq, k_new, v_new, cache_k, cache_v)


def _sample_finish(x, yc, o1, o2, lam_params, g, mod_rows, w_out, ln1, w_gate, w_up, w_down, ln2, alpha, lam_init):
    n, d = x.shape
    kern = functools.partial(_sample_finish_kernel, alpha=alpha, lam_init=lam_init)
    args = (x, yc, o1, o2, lam_params, g, mod_rows, w_out, ln1, w_gate, w_up, w_down, ln2)
    return pl.pallas_call(
        kern,
        grid=(1,),
        in_specs=[_resident(a.shape) for a in args],
        out_specs=pl.BlockSpec((n, d), lambda i: (0, 0)),
        out_shape=jax.ShapeDtypeStruct((n, d), F32),
        compiler_params=_params(1),
        name="sample_finish",
    )(*args)


def _sample_in_kernel(x_ref, mod_ref, w_ref, cw_ref, st1_ref, st2_ref, tab_ref,
                      yconv_ref, q_ref, k_ref, v_ref, u_ref, *, t_len, cc):
    xm = (x_ref[...] * (1.0 + mod_ref[1]) + mod_ref[0]).astype(BF16)
    z = jnp.dot(xm, w_ref[...], preferred_element_type=F32)
    h, gb, gc = z[:, 0:cc], z[:, cc:2 * cc], z[:, 2 * cc:3 * cc]
    u = gc * h
    n = u.shape[0]
    t = lax.broadcasted_iota(jnp.int32, (n, cc), 0) % t_len
    u1 = jnp.where(t >= 1, pltpu.roll(u, 1, 0), st1_ref[...])
    u2 = jnp.where(t >= 2, pltpu.roll(u, 2, 0), st2_ref[...])
    cw = cw_ref[...]
    yconv_ref[...] = (gb * (cw[0:1] * u2 + cw[1:2] * u1 + cw[2:3] * u)).astype(BF16)
    u_ref[...] = u
    hw = N_HEADS * HEAD_W
    q_ref[...] = (_rope(z[:, 3 * cc:3 * cc + hw], tab_ref) * Q_SCALE).astype(BF16)
    k_ref[...] = _rope(z[:, 3 * cc + hw:3 * cc + 2 * hw], tab_ref)
    v_ref[...] = z[:, 3 * cc + 2 * hw:]


def _sample_in(x, mod_rows, w_in, conv_w, st1, st2, tab, t_len):
    n, _ = x.shape
    cc = conv_w.shape[1]
    hw = N_HEADS * HEAD_W
    kern = functools.partial(_sample_in_kernel, t_len=t_len, cc=cc)
    args = (x, mod_rows, w_in, conv_w, st1, st2, tab)
    out_shapes = [jax.ShapeDtypeStruct((n, cc), BF16), jax.ShapeDtypeStruct((n, hw), BF16),
                  jax.ShapeDtypeStruct((n, hw), F32), jax.ShapeDtypeStruct((n, N_HEADS * V_HEAD), F32),
                  jax.ShapeDtypeStruct((n, cc), F32)]
    return pl.pallas_call(
        kern,
        grid=(1,),
        in_specs=[_resident(a.shape) for a in args],
        out_specs=[pl.BlockSpec(o.shape, lambda i: (0, 0)) for o in out_shapes],
        out_shape=out_shapes,
        compiler_params=_params(1),
        name="sample_in",
    )(*args)


def _query_rows(q, bsz, t_len):
    q5 = q.reshape(bsz, t_len, N_HEADS, 2, QK_HEAD).transpose(0, 2, 3, 1, 4)
    eye_c = jnp.eye(2, dtype=q.dtype)
    w = q5[:, :, :, :, None, :] * eye_c[None, None, :, None, :, None]
    return w.reshape(bsz, N_HEADS * 2 * t_len, HEAD_W)


def _token_head_rows(a, bsz, t_len, rows):
    a = a.reshape(bsz, t_len * N_HEADS, a.shape[1] // N_HEADS)
    return jnp.pad(a, ((0, 0), (0, rows - t_len * N_HEADS), (0, 0)))


def _component_outputs(o, bsz, t_len):
    o5 = o.reshape(bsz, N_HEADS, 2, t_len, V_HEAD).transpose(2, 0, 3, 1, 4)
    o5 = o5.reshape(2, bsz * t_len, N_HEADS * V_HEAD)
    return o5[0], o5[1]


def kernel(x_prompt, x_sample, c_prompt, c_sample, cache_k, cache_v, state_conv, page_table, w_ada, b_ada, w_in, conv_w, lambda_q1, lambda_k1, lambda_q2, lambda_k2, subln_g, w_out, ln1_g, ln1_b, w_gate, w_up, w_down, ln2_g, ln2_b):
    bp, s_len, d = x_prompt.shape
    bs, t_len, _ = x_sample.shape
    depth, n_pool, page = cache_k.shape[:3]
    past = page_table.shape[1] * page
    cc = conv_w.shape[2]
    alpha = (2.0 * depth) ** 0.25
    assert (bs * t_len) % SUBLANES == 0 and t_len <= page

    ts = min(1024, s_len)
    tq = min(512, s_len)
    tm = min(512, s_len)

    tab_p = _rope_tables(jnp.arange(s_len))
    tab_s = jnp.tile(_rope_tables(past + jnp.arange(t_len)), (1, bs, 1))
    ck = cache_k.reshape(depth * n_pool, page * N_HEADS, HEAD_W)
    cv = cache_v.reshape(depth * n_pool, page * N_HEADS, V_HEAD)
    c_all = jnp.concatenate([c_prompt, c_sample], axis=0)

    xp, xs = x_prompt, x_sample.reshape(bs * t_len, d)
    kp_l, vp_l, cp_l, ks_l, vs_l, cs_l = [], [], [], [], [], []
    for l in range(depth):
        lam_init = 0.8 - 0.6 * math.exp(-0.3 * l)
        w_in_b, w_out_b = w_in[l].astype(BF16), w_out[l].astype(BF16)
        w_gate_b, w_up_b, w_down_b = w_gate[l].astype(BF16), w_up[l].astype(BF16), w_down[l].astype(BF16)
        lam_params = jnp.stack([lambda_q1[l], lambda_k1[l], lambda_q2[l], lambda_k2[l]])
        g = subln_g[l].reshape(1, V_HEAD)
        ln1 = jnp.stack([ln1_g[l], ln1_b[l]])
        ln2 = jnp.stack([ln2_g[l], ln2_b[l]])

        mod = _adaln(c_all, w_ada[l], b_ada[l]).reshape(bp + bs, N_MOD, d)
        mod_p = mod[:bp]
        mod_s = jnp.repeat(mod[bp:].transpose(1, 0, 2), t_len, axis=1)

        yconv, q, kb, vt, k, v, cst = _prompt_in(xp, mod_p, w_in_b, conv_w[l], tab_p, ts)
        yattn = _prompt_attn(q, kb, vt, lam_params, g, lam_init, tq)
        kp_l.append(k.reshape(bp, s_len, N_HEADS, HEAD_W))
        vp_l.append(v.reshape(bp, s_len, N_HEADS, V_HEAD))
        cp_l.append(cst[-1])

        st = state_conv[l]
        zero = jnp.zeros((bs, 1, cc), F32)
        st1 = jnp.concatenate([st[:, 1:2]] + [zero] * (t_len - 1), axis=1).reshape(bs * t_len, cc)
        st2 = jnp.concatenate([st[:, 0:1], st[:, 1:2]] + [zero] * (t_len - 2), axis=1).reshape(bs * t_len, cc)
        yconv_s, q_s, k_s, v_s, u_s = _sample_in(xs, mod_s, w_in_b, conv_w[l], st1, st2, tab_s, t_len)
        wq = _query_rows(q_s, bs, t_len)
        rows = pl.cdiv(t_len * N_HEADS, LANES) * LANES

        xp, o = _finish_attn(xp, yconv, yattn, mod_p, w_out_b, ln1, w_gate_b, w_up_b, w_down_b, ln2, alpha, tm,
                             page_table, wq, _token_head_rows(k_s, bs, t_len, rows),
                             _token_head_rows(v_s, bs, t_len, rows), ck, cv, l * n_pool, t_len)
        o1, o2 = _component_outputs(o, bs, t_len)
        xs = _sample_finish(xs, yconv_s, o1, o2, lam_params, g, mod_s, w_out_b, ln1, w_gate_b, w_up_b, w_down_b,
                            ln2, alpha, lam_init)
        ks_l.append(k_s.reshape(bs, t_len, N_HEADS, HEAD_W))
        vs_l.append(v_s.reshape(bs, t_len, N_HEADS, V_HEAD))
        cs_l.append(u_s.reshape(bs, t_len, cc)[:, t_len - (CONV_K - 1):])

    return (xp, xs.reshape(bs, t_len, d), jnp.stack(kp_l), jnp.stack(vp_l), jnp.stack(cp_l),
            jnp.stack(ks_l), jnp.stack(vs_l), jnp.stack(cs_l))
```

```python
import functools
import math

import jax
import jax.numpy as jnp
import numpy as np
from jax import lax
from jax.experimental import pallas as pl
from jax.experimental.pallas import tpu as pltpu

F32 = jnp.float32
BF16 = jnp.bfloat16

N_HEADS = 4
QK_HEAD = 64
V_HEAD = 128
HEAD_W = 2 * QK_HEAD
CONV_K = 3
ROT_DIM = QK_HEAD // 4
ROPE_THETA = 500000.0
N_MOD = 6
LN_EPS = 1e-5
LANES = 128
SUBLANES = 8
SUM_ROWS = 16
VMEM_LIMIT_BYTES = 56 * 1024 * 1024
NEG_INF = float("-inf")
Q_SCALE = QK_HEAD ** -0.5 * math.log2(math.e)


def _params(n_axes):
    return pltpu.CompilerParams(dimension_semantics=("arbitrary",) * n_axes,
                                vmem_limit_bytes=VMEM_LIMIT_BYTES)


def _resident(shape):
    return pl.BlockSpec(shape, lambda *_: (0,) * len(shape), pipeline_mode=pl.Buffered(1))


def _largest_divisor(n, cap):
    return max(d for d in range(1, min(n, cap) + 1) if n % d == 0)


def _layer_norm(x, g, b):
    mu = jnp.mean(x, axis=-1, keepdims=True)
    xc = x - mu
    var = jnp.mean(xc * xc, axis=-1, keepdims=True)
    return xc * lax.rsqrt(var + LN_EPS) * g + b


def _adaln_kernel(c_ref, w_ref, b_ref, o_ref):
    a = jax.nn.silu(c_ref[...]).astype(BF16)
    o_ref[...] = jnp.dot(a, w_ref[...].astype(BF16), preferred_element_type=F32) + b_ref[...]


def _adaln(c, w, b):
    n, d = c.shape
    n_out = w.shape[1]
    tn = 1536 if n_out % 1536 == 0 else n_out
    return pl.pallas_call(
        _adaln_kernel,
        grid=(n_out // tn,),
        in_specs=[pl.BlockSpec((n, d), lambda j: (0, 0)),
                  pl.BlockSpec((d, tn), lambda j: (0, j)),
                  pl.BlockSpec((1, tn), lambda j: (0, j))],
        out_specs=pl.BlockSpec((n, tn), lambda j: (0, j)),
        out_shape=jax.ShapeDtypeStruct((n, n_out), F32),
        compiler_params=_params(1),
        name="adaln",
    )(c, w, b.reshape(1, n_out))


def _rope_tables(pos):
    inv = ROPE_THETA ** (-np.arange(0, ROT_DIM, 2, dtype=np.float64) / ROT_DIM)
    ang = np.asarray(pos, np.float64)[:, None] * inv[None, :]
    cos, sin = np.cos(ang), np.sin(ang)
    t, half = ang.shape[0], ROT_DIM // 2
    c = np.concatenate([cos, cos, np.ones((t, QK_HEAD - ROT_DIM))], axis=1)
    s_up = np.concatenate([-sin, np.zeros((t, QK_HEAD - half))], axis=1)
    s_dn = np.concatenate([np.zeros((t, half)), sin, np.zeros((t, QK_HEAD - ROT_DIM))], axis=1)
    return np.stack([np.tile(a, (1, LANES // QK_HEAD)) for a in (c, s_up, s_dn)]).astype(np.float32)


def _rope(x, tab_ref):
    c, s_up, s_dn = tab_ref[0], tab_ref[1], tab_ref[2]
    half = ROT_DIM // 2
    cols = []
    for i in range(x.shape[1] // LANES):
        xc = x[:, i * LANES:(i + 1) * LANES]
        cols.append(xc * c + pltpu.roll(xc, LANES - half, 1) * s_up + pltpu.roll(xc, half, 1) * s_dn)
    return jnp.concatenate(cols, axis=1)


def _prompt_in_kernel(x_ref, mod_ref, w_ref, cw_ref, tab_ref,
                      yconv_ref, q_ref, kb_ref, vt_ref, k_ref, v_ref, cst_ref, carry_ref, ubuf_ref, *, ts, cc):
    s, b = pl.program_id(0), pl.program_id(1)

    @pl.when(s == 0)
    def _():
        carry_ref[b] = jnp.zeros((SUBLANES, cc), F32)

    mod = mod_ref[0]
    xm = (x_ref[0] * (1.0 + mod[1:2]) + mod[0:1]).astype(BF16)

    z = jnp.dot(xm, w_ref[:, 0:3 * cc], preferred_element_type=F32)
    h, gb, gc = z[:, 0:cc], z[:, cc:2 * cc], z[:, 2 * cc:3 * cc]
    u = gc * h

    ubuf_ref[0:SUBLANES] = carry_ref[b]
    ubuf_ref[SUBLANES:SUBLANES + ts] = u
    u1 = ubuf_ref[SUBLANES - 1:SUBLANES - 1 + ts]
    u2 = ubuf_ref[SUBLANES - 2:SUBLANES - 2 + ts]
    cw = cw_ref[...]
    yconv_ref[0] = (gb * (cw[0:1] * u2 + cw[1:2] * u1 + cw[2:3] * u)).astype(BF16)
    carry_ref[b] = u[ts - SUBLANES:]
    cst_ref[0, 0] = u[ts - (CONV_K - 1):]

    z = jnp.dot(xm, w_ref[:, 3 * cc:], preferred_element_type=F32)
    hw = N_HEADS * HEAD_W
    q_ref[0] = (_rope(z[:, 0:hw], tab_ref) * Q_SCALE).astype(BF16)
    k = _rope(z[:, hw:2 * hw], tab_ref)
    v = z[:, 2 * hw:]
    kb_ref[0] = k.astype(BF16)
    vt_ref[0] = v.T.astype(BF16)
    for h in range(N_HEADS):
        rows = pl.ds(h, ts, stride=N_HEADS)
        k_ref[0, rows, :] = k[:, h * HEAD_W:(h + 1) * HEAD_W]
        v_ref[0, rows, :] = v[:, h * V_HEAD:(h + 1) * V_HEAD]


def _prompt_in(x, mod, w_in, conv_w, tab, ts):
    bsz, seq, d = x.shape
    cc = conv_w.shape[1]
    hw = N_HEADS * HEAD_W
    n_in = w_in.shape[1]
    tok = lambda s, b: (b, s, 0)
    kern = functools.partial(_prompt_in_kernel, ts=ts, cc=cc)
    return pl.pallas_call(
        kern,
        grid=(seq // ts, bsz),
        in_specs=[pl.BlockSpec((1, ts, d), tok),
                  pl.BlockSpec((1, N_MOD, d), lambda s, b: (b, 0, 0)),
                  _resident((d, n_in)),
                  _resident((CONV_K, cc)),
                  pl.BlockSpec((3, ts, LANES), lambda s, b: (0, s, 0))],
        out_specs=[pl.BlockSpec((1, ts, cc), tok),
                   pl.BlockSpec((1, ts, hw), tok),
                   pl.BlockSpec((1, ts, hw), tok),
                   pl.BlockSpec((1, N_HEADS * V_HEAD, ts), lambda s, b: (b, 0, s)),
                   pl.BlockSpec((1, ts * N_HEADS, HEAD_W), tok),
                   pl.BlockSpec((1, ts * N_HEADS, V_HEAD), tok),
                   pl.BlockSpec((1, 1, CONV_K - 1, cc), lambda s, b: (s, b, 0, 0))],
        out_shape=[jax.ShapeDtypeStruct((bsz, seq, cc), BF16),
                   jax.ShapeDtypeStruct((bsz, seq, hw), BF16),
                   jax.ShapeDtypeStruct((bsz, seq, hw), BF16),
                   jax.ShapeDtypeStruct((bsz, N_HEADS * V_HEAD, seq), BF16),
                   jax.ShapeDtypeStruct((bsz, seq * N_HEADS, HEAD_W), F32),
                   jax.ShapeDtypeStruct((bsz, seq * N_HEADS, V_HEAD), F32),
                   jax.ShapeDtypeStruct((seq // ts, bsz, CONV_K - 1, cc), F32)],
        scratch_shapes=[pltpu.VMEM((bsz, SUBLANES, cc), F32),
                        pltpu.VMEM((ts + SUBLANES, cc), F32)],
        compiler_params=_params(2),
        name="prompt_in",
    )(x, mod, w_in, conv_w, tab)


def _lambda_value(lam_ref, lam_init):
    lp = lam_ref[...]
    a = jnp.sum(lp[0:1] * lp[1:2], axis=1, keepdims=True)
    b = jnp.sum(lp[2:3] * lp[3:4], axis=1, keepdims=True)
    return jnp.exp(a) - jnp.exp(b) + lam_init


def _head_out(o1, o2, lam, g, lam_init):
    d = o1 - lam * o2
    d = d * lax.rsqrt(jnp.mean(d * d, axis=-1, keepdims=True) + LN_EPS) * g
    return d * (1.0 - lam_init)


def _softmax_step(carry, s, v):
    m, l, acc = carry
    m_new = jnp.maximum(m, jnp.max(s, axis=1, keepdims=True))
    alpha = jnp.exp2(m - m_new)
    p = jnp.exp2(s - m_new)
    l = alpha * l + jnp.sum(p, axis=1, keepdims=True)
    acc = alpha * acc + jnp.dot(p.astype(BF16), v, preferred_element_type=F32)
    return m_new, l, acc


def _prompt_attn_kernel(q_ref, k_ref, vt_ref, lam_ref, g_ref, o_ref, *, seq, tq, lam_init):
    lam = _lambda_value(lam_ref, lam_init)
    lane = lax.broadcasted_iota(jnp.int32, (tq, HEAD_W), 1)
    nt = (((1,), (1,)), ((), ()))
    g_col = g_ref[...]

    def stacked_queries(qi):
        q = q_ref[0, qi * tq:(qi + 1) * tq, :]
        zero = jnp.zeros_like(q)
        return jnp.concatenate([jnp.where(lane < QK_HEAD, q, zero), jnp.where(lane >= QK_HEAD, q, zero)], axis=0)

    ones_rows = jnp.ones((SUM_ROWS, tq), BF16)

    def step(carry, qq, qi, ki):
        m, acc = carry
        keys = slice(ki * tq, (ki + 1) * tq)
        s = lax.dot_general(k_ref[0, keys, :], qq, nt, preferred_element_type=F32)
        if ki == qi:
            key = lax.broadcasted_iota(jnp.int32, s.shape, 0)
            col = lax.broadcasted_iota(jnp.int32, s.shape, 1)
            s = jnp.where(key <= jnp.where(col >= tq, col - tq, col), s, NEG_INF)
        m_new = jnp.maximum(m, jnp.max(s, axis=0, keepdims=True))
        p = jnp.exp2(s - m_new).astype(BF16)
        vt_ones = jnp.concatenate([vt_ref[0, :, keys], ones_rows], axis=0)
        acc = jnp.exp2(m - m_new) * acc + jnp.dot(vt_ones, p, preferred_element_type=F32)
        return m_new, acc

    for qi in range(seq // tq):
        qq = stacked_queries(qi)
        carry = (jnp.full((1, 2 * tq), NEG_INF, F32), jnp.zeros((V_HEAD + SUM_ROWS, 2 * tq), F32))
        for ki in range(qi + 1):
            carry = step(carry, qq, qi, ki)
        _, acc = carry
        o = acc[:V_HEAD] / acc[V_HEAD:V_HEAD + 1]
        d = o[:, :tq] - lam * o[:, tq:]
        d = d * lax.rsqrt(jnp.mean(d * d, axis=0, keepdims=True) + LN_EPS) * g_col * (1.0 - lam_init)
        o_ref[0, qi * tq:(qi + 1) * tq, :] = d.T.astype(BF16)


def _prompt_attn(q, k, vt, lam_params, g, lam_init, tq):
    bsz, seq, _ = q.shape
    blk = pl.BlockSpec((1, seq, HEAD_W), lambda b, h: (b, 0, h))
    kern = functools.partial(_prompt_attn_kernel, seq=seq, tq=tq, lam_init=lam_init)
    return pl.pallas_call(
        kern,
        grid=(bsz, N_HEADS),
        in_specs=[blk, blk,
                  pl.BlockSpec((1, V_HEAD, seq), lambda b, h: (b, h, 0)),
                  pl.BlockSpec((4, QK_HEAD), lambda b, h: (0, 0)),
                  pl.BlockSpec((V_HEAD, 1), lambda b, h: (0, 0))],
        out_specs=blk,
        out_shape=jax.ShapeDtypeStruct((bsz, seq, N_HEADS * V_HEAD), BF16),
        compiler_params=_params(2),
        name="prompt_attn",
    )(q, k, vt, lam_params, g.reshape(V_HEAD, 1))


FINISH_PHASES = 4
SAMPLE_CHUNK_PAGES = 16
SAMPLE_RING_SLOTS = 2


def _finish_math(x, yc, ya, gate1, shift2, scale2, gate2, wo_ref, ln1g, ln1b, wg_ref, wu_ref, wd_ref,
                 ln2g, ln2b, alpha, around_phase=None):
    run = around_phase or (lambda i, phase: phase())
    cc = yc.shape[1]
    v = {}

    def project():
        y = (jnp.dot(yc, wo_ref[0:cc], preferred_element_type=F32)
             + jnp.dot(ya, wo_ref[cc:], preferred_element_type=F32))
        v["x1"] = _layer_norm(alpha * x + gate1 * y, ln1g, ln1b)
        v["hf"] = (v["x1"] * (1.0 + scale2) + shift2).astype(BF16)

    def gate():
        v["g"] = jnp.dot(v["hf"], wg_ref[...], preferred_element_type=F32)

    def up():
        u = jnp.dot(v["hf"], wu_ref[...], preferred_element_type=F32)
        v["act"] = (jax.nn.silu(v["g"]) * u).astype(BF16)

    def down():
        f = jnp.dot(v["act"], wd_ref[...], preferred_element_type=F32)
        v["out"] = _layer_norm(alpha * v["x1"] + gate2 * f, ln2g, ln2b)

    for i, phase in enumerate((project, gate, up, down)):
        run(i, phase)
    return v["out"]


def _finish_attn_kernel(pt_ref, x_ref, yc_ref, ya_ref, mod_ref, wo_ref, ln1_ref, wg_ref, wu_ref, wd_ref, ln2_ref,
                        wq_ref, kn_ref, vn_ref, ck_ref, cv_ref, o_ref, so_ref,
                        kbuf, vbuf, sem, m_ref, l_ref, acc_ref,
                        *, alpha, n_pages, pages_per_chunk, chunks_per_step, page_base, t_len):
    t = pl.program_id(0) * pl.num_programs(1) + pl.program_id(1)
    n_chunks = n_pages // pages_per_chunk
    total_chunks = pl.num_programs(0) * pl.num_programs(1) * chunks_per_step
    n_slots = kbuf.shape[0]
    page_rows = ck_ref.shape[1]
    nt = (((1,), (1,)), ((), ()))
    rows = wq_ref.shape[1]

    def copies(index):
        batch, chunk, slot = index // n_chunks, index % n_chunks, index % n_slots
        out = []
        for g in range(pages_per_chunk):
            pg = pt_ref[batch, chunk * pages_per_chunk + g] + page_base
            dst = pl.ds(g * page_rows, page_rows)
            out.append((pltpu.make_async_copy(ck_ref.at[pg], kbuf.at[slot, dst], sem.at[0, slot]),
                        pltpu.make_async_copy(cv_ref.at[pg], vbuf.at[slot, dst], sem.at[1, slot])))
        return out

    def start(index):
        for k_copy, v_copy in copies(index):
            k_copy.start(priority=0)
            v_copy.start(priority=1)

    def reset_state():
        m_ref[...] = jnp.full(m_ref.shape, NEG_INF, F32)
        l_ref[...] = jnp.zeros(l_ref.shape, F32)
        acc_ref[...] = jnp.zeros(acc_ref.shape, F32)

    @pl.when(t == 0)
    def _():
        reset_state()
        for index in range(n_slots):
            start(index)

    row_head = lax.broadcasted_iota(jnp.int32, (rows, page_rows), 0) // (2 * t_len)
    col_head = lax.broadcasted_iota(jnp.int32, (rows, page_rows), 1) % N_HEADS
    own_head = row_head == col_head

    def attend(s, v):
        m, l, acc = _softmax_step((m_ref[...], l_ref[...], acc_ref[...]), s, v)
        m_ref[...] = m
        l_ref[...] = l
        acc_ref[...] = acc

    def consume(index):
        slot = index % n_slots
        for k_copy, v_copy in copies(index):
            k_copy.wait()
            v_copy.wait()
        head_bias = jnp.where(own_head, 0.0, NEG_INF)
        chunk_bias = jnp.concatenate([head_bias] * pages_per_chunk, axis=1)
        s = lax.dot_general(wq_ref[0], kbuf[slot].astype(BF16), nt, preferred_element_type=F32) + chunk_bias
        attend(s, vbuf[slot].astype(BF16))

    def refill(index):
        @pl.when(index + n_slots < total_chunks)
        def _():
            start(index + n_slots)

    def around_phase(i, matmul_phase):
        for k in range(chunks_per_step):
            if (0 if 2 * k < chunks_per_step else FINISH_PHASES // 2) == i:
                consume(t * chunks_per_step + k)
                refill(t * chunks_per_step + k)
        matmul_phase()

    mod = mod_ref[0]
    o_ref[0] = _finish_math(x_ref[0], yc_ref[0], ya_ref[0], mod[2:3], mod[3:4], mod[4:5], mod[5:6],
                            wo_ref, ln1_ref[0:1], ln1_ref[1:2], wg_ref, wu_ref, wd_ref,
                            ln2_ref[0:1], ln2_ref[1:2], alpha, around_phase)

    @pl.when(((t + 1) * chunks_per_step) % n_chunks == 0)
    def _():
        s = lax.dot_general(wq_ref[0], kn_ref[0].astype(BF16), nt, preferred_element_type=F32)
        new_rows = kn_ref.shape[1]
        q_step = lax.broadcasted_iota(jnp.int32, (rows, new_rows), 0) % t_len
        k_step = lax.broadcasted_iota(jnp.int32, (rows, new_rows), 1) // N_HEADS
        s = jnp.where(own_head[:, :new_rows] & (k_step <= q_step), s, NEG_INF)
        attend(s, vn_ref[0].astype(BF16))
        so_ref[0] = acc_ref[...] / l_ref[...]
        reset_state()


def _sample_finish_kernel(x_ref, yc_ref, o1_ref, o2_ref, lam_ref, g_ref, mod_ref, wo_ref, ln1_ref,
                          wg_ref, wu_ref, wd_ref, ln2_ref, o_ref, *, alpha, lam_init):
    lam = _lambda_value(lam_ref, lam_init)
    o1, o2 = o1_ref[...], o2_ref[...]
    ya = jnp.concatenate(
        [_head_out(o1[:, h * V_HEAD:(h + 1) * V_HEAD], o2[:, h * V_HEAD:(h + 1) * V_HEAD], lam, g_ref[...], lam_init)
         for h in range(N_HEADS)], axis=1).astype(BF16)
    d = x_ref.shape[1]
    gate1, shift2, scale2, gate2 = (mod_ref[:, i * d:(i + 1) * d] for i in range(2, N_MOD))
    o_ref[...] = _finish_math(x_ref[...], yc_ref[...], ya, gate1, shift2, scale2, gate2,
                              wo_ref, ln1_ref[0:1], ln1_ref[1:2], wg_ref, wu_ref, wd_ref,
                              ln2_ref[0:1], ln2_ref[1:2], alpha)


def _finish_attn(x, yc, ya, mod, w_out, ln1, w_gate, w_up, w_down, ln2, alpha, tm,
                 page_table, wq, k_new, v_new, cache_k, cache_v, page_base, t_len):
    bsz, seq, d = x.shape
    cc, d_ff = yc.shape[2], w_gate.shape[1]
    n_s = seq // tm
    sample_bsz, n_pages = page_table.shape
    _, page_rows, width = cache_k.shape
    rows = wq.shape[1]
    pages_per_step, ragged = divmod(sample_bsz * n_pages, bsz * n_s)
    assert not ragged and pages_per_step > 0
    g = _largest_divisor(math.gcd(pages_per_step, n_pages), SAMPLE_CHUNK_PAGES)
    chunks_per_step, n_chunks = pages_per_step // g, n_pages // g
    n_slots = SAMPLE_RING_SLOTS
    assert bsz * n_s * chunks_per_step >= n_slots
    kern = functools.partial(_finish_attn_kernel, alpha=alpha, n_pages=n_pages, pages_per_chunk=g,
                             chunks_per_step=chunks_per_step, page_base=page_base, t_len=t_len)
    tok = lambda b, s, pt: (b, s, 0)
    fixed = lambda shape: pl.BlockSpec(shape, lambda b, s, pt: (0,) * len(shape), pipeline_mode=pl.Buffered(1))
    sample = lambda r: pl.BlockSpec(
        (1, r, width), lambda b, s, pt: (((b * n_s + s) * chunks_per_step) // n_chunks, 0, 0))
    grid_spec = pltpu.PrefetchScalarGridSpec(
        num_scalar_prefetch=1,
        grid=(bsz, n_s),
        in_specs=[pl.BlockSpec((1, tm, d), tok),
                  pl.BlockSpec((1, tm, cc), tok),
                  pl.BlockSpec((1, tm, d - cc), tok),
                  pl.BlockSpec((1, N_MOD, d), lambda b, s, pt: (b, 0, 0)),
                  fixed((d, d)), fixed((2, d)),
                  fixed((d, d_ff)), fixed((d, d_ff)), fixed((d_ff, d)), fixed((2, d)),
                  sample(rows), sample(k_new.shape[1]), sample(v_new.shape[1]),
                  pl.BlockSpec(memory_space=pl.ANY),
                  pl.BlockSpec(memory_space=pl.ANY)],
        out_specs=[pl.BlockSpec((1, tm, d), tok), sample(rows)],
        scratch_shapes=[pltpu.VMEM((n_slots, g * page_rows, width), F32),
                        pltpu.VMEM((n_slots, g * page_rows, width), F32),
                        pltpu.SemaphoreType.DMA((2, n_slots)),
                        pltpu.VMEM((rows, 1), F32), pltpu.VMEM((rows, 1), F32), pltpu.VMEM((rows, V_HEAD), F32)],
    )
    return pl.pallas_call(
        kern,
        grid_spec=grid_spec,
        out_shape=[jax.ShapeDtypeStruct((bsz, seq, d), F32),
                   jax.ShapeDtypeStruct((sample_bsz, rows, width), F32)],
        compiler_params=_params(2),
        name="finish_attn",
    )(page_table, x, yc, ya, mod, w_out, ln1, w_gate, w_up, w_down, ln2, wq, k_new, v_new, cache_k, cache_v)


def _sample_finish(x, yc, o1, o2, lam_params, g, mod_rows, w_out, ln1, w_gate, w_up, w_down, ln2, alpha, lam_init):
    n, d = x.shape
    kern = functools.partial(_sample_finish_kernel, alpha=alpha, lam_init=lam_init)
    args = (x, yc, o1, o2, lam_params, g, mod_rows, w_out, ln1, w_gate, w_up, w_down, ln2)
    return pl.pallas_call(
        kern,
        grid=(1,),
        in_specs=[_resident(a.shape) for a in args],
        out_specs=pl.BlockSpec((n, d), lambda i: (0, 0)),
        out_shape=jax.ShapeDtypeStruct((n, d), F32),
        compiler_params=_params(1),
        name="sample_finish",
    )(*args)


def _sample_in_kernel(x_ref, mod_ref, w_ref, cw_ref, st1_ref, st2_ref, tab_ref,
                      yconv_ref, q_ref, k_ref, v_ref, u_ref, *, t_len, cc):
    d = x_ref.shape[1]
    xm = (x_ref[...] * (1.0 + mod_ref[:, d:2 * d]) + mod_ref[:, 0:d]).astype(BF16)
    z = jnp.dot(xm, w_ref[...], preferred_element_type=F32)
    h, gb, gc = z[:, 0:cc], z[:, cc:2 * cc], z[:, 2 * cc:3 * cc]
    u = gc * h
    n = u.shape[0]
    t = lax.broadcasted_iota(jnp.int32, (n, cc), 0) % t_len
    u1 = jnp.where(t >= 1, pltpu.roll(u, 1, 0), st1_ref[...])
    u2 = jnp.where(t >= 2, pltpu.roll(u, 2, 0), st2_ref[...])
    cw = cw_ref[...]
    yconv_ref[...] = (gb * (cw[0:1] * u2 + cw[1:2] * u1 + cw[2:3] * u)).astype(BF16)
    u_ref[...] = u
    hw = N_HEADS * HEAD_W
    q_ref[...] = (_rope(z[:, 3 * cc:3 * cc + hw], tab_ref) * Q_SCALE).astype(BF16)
    k_ref[...] = _rope(z[:, 3 * cc + hw:3 * cc + 2 * hw], tab_ref)
    v_ref[...] = z[:, 3 * cc + 2 * hw:]


def _sample_in(x, mod_rows, w_in, conv_w, st1, st2, tab, t_len):
    n, _ = x.shape
    cc = conv_w.shape[1]
    hw = N_HEADS * HEAD_W
    kern = functools.partial(_sample_in_kernel, t_len=t_len, cc=cc)
    args = (x, mod_rows, w_in, conv_w, st1, st2, tab)
    out_shapes = [jax.ShapeDtypeStruct((n, cc), BF16), jax.ShapeDtypeStruct((n, hw), BF16),
                  jax.ShapeDtypeStruct((n, hw), F32), jax.ShapeDtypeStruct((n, N_HEADS * V_HEAD), F32),
                  jax.ShapeDtypeStruct((n, cc), F32)]
    return pl.pallas_call(
        kern,
        grid=(1,),
        in_specs=[_resident(a.shape) for a in args],
        out_specs=[pl.BlockSpec(o.shape, lambda i: (0, 0)) for o in out_shapes],
        out_shape=out_shapes,
        compiler_params=_params(1),
        name="sample_in",
    )(*args)


def _query_rows(q, bsz, t_len):
    q5 = q.reshape(bsz, t_len, N_HEADS, 2, QK_HEAD).transpose(0, 2, 3, 1, 4)
    eye_c = jnp.eye(2, dtype=q.dtype)
    w = q5[:, :, :, :, None, :] * eye_c[None, None, :, None, :, None]
    return w.reshape(bsz, N_HEADS * 2 * t_len, HEAD_W)


def _token_head_rows(a, bsz, t_len, rows):
    a = a.reshape(bsz, t_len * N_HEADS, a.shape[1] // N_HEADS)
    return jnp.pad(a, ((0, 0), (0, rows - t_len * N_HEADS), (0, 0)))


def _component_outputs(o, bsz, t_len):
    o5 = o.reshape(bsz, N_HEADS, 2, t_len, V_HEAD).transpose(2, 0, 3, 1, 4)
    o5 = o5.reshape(2, bsz * t_len, N_HEADS * V_HEAD)
    return o5[0], o5[1]


def kernel(x_prompt, x_sample, c_prompt, c_sample, cache_k, cache_v, state_conv, page_table, w_ada, b_ada, w_in, conv_w, lambda_q1, lambda_k1, lambda_q2, lambda_k2, subln_g, w_out, ln1_g, ln1_b, w_gate, w_up, w_down, ln2_g, ln2_b):
    bp, s_len, d = x_prompt.shape
    bs, t_len, _ = x_sample.shape
    depth, n_pool, page = cache_k.shape[:3]
    past = page_table.shape[1] * page
    cc = conv_w.shape[2]
    alpha = (2.0 * depth) ** 0.25
    assert (bs * t_len) % SUBLANES == 0 and t_len <= page

    ts = min(1024, s_len)
    tq = min(512, s_len)
    tm = min(512, s_len)

    tab_p = jnp.asarray(_rope_tables(np.arange(s_len)))
    tab_s = jnp.asarray(np.tile(_rope_tables(past + np.arange(t_len)), (1, bs, 1)))
    ck = cache_k.reshape(depth * n_pool, page * N_HEADS, HEAD_W)
    cv = cache_v.reshape(depth * n_pool, page * N_HEADS, V_HEAD)
    c_all = jnp.concatenate([c_prompt, jnp.repeat(c_sample, t_len, axis=0)], axis=0)

    xp, xs = x_prompt, x_sample.reshape(bs * t_len, d)
    kp_l, vp_l, cp_l, ks_l, vs_l, cs_l = [], [], [], [], [], []
    for l in range(depth):
        lam_init = 0.8 - 0.6 * math.exp(-0.3 * l)
        w_in_b, w_out_b = w_in[l].astype(BF16), w_out[l].astype(BF16)
        w_gate_b, w_up_b, w_down_b = w_gate[l].astype(BF16), w_up[l].astype(BF16), w_down[l].astype(BF16)
        lam_params = jnp.stack([lambda_q1[l], lambda_k1[l], lambda_q2[l], lambda_k2[l]])
        g = subln_g[l].reshape(1, V_HEAD)
        ln1 = jnp.stack([ln1_g[l], ln1_b[l]])
        ln2 = jnp.stack([ln2_g[l], ln2_b[l]])

        mod = _adaln(c_all, w_ada[l], b_ada[l])
        mod_p = mod[:bp].reshape(bp, N_MOD, d)
        mod_s = mod[bp:]

        yconv, q, kb, vt, k, v, cst = _prompt_in(xp, mod_p, w_in_b, conv_w[l], tab_p, ts)
        yattn = _prompt_attn(q, kb, vt, lam_params, g, lam_init, tq)
        kp_l.append(k.reshape(bp, s_len, N_HEADS, HEAD_W))
        vp_l.append(v.reshape(bp, s_len, N_HEADS, V_HEAD))
        cp_l.append(cst[-1])

        st = state_conv[l]
        zero = jnp.zeros((bs, 1, cc), F32)
        st1 = jnp.concatenate([st[:, 1:2]] + [zero] * (t_len - 1), axis=1).reshape(bs * t_len, cc)
        st2 = jnp.concatenate([st[:, 0:1], st[:, 1:2]] + [zero] * (t_len - 2), axis=1).reshape(bs * t_len, cc)
        yconv_s, q_s, k_s, v_s, u_s = _sample_in(xs, mod_s, w_in_b, conv_w[l], st1, st2, tab_s, t_len)
        wq = _query_rows(q_s, bs, t_len)
        rows = pl.cdiv(t_len * N_HEADS, LANES) * LANES

        xp, o = _finish_attn(xp, yconv, yattn, mod_p, w_out_b, ln1, w_gate_b, w_up_b, w_down_b, ln2, alpha, tm,
                             page_table, wq, _token_head_rows(k_s, bs, t_len, rows),
                             _token_head_rows(v_s, bs, t_len, rows), ck, cv, l * n_pool, t_len)
        o1, o2 = _component_outputs(o, bs, t_len)
        xs = _sample_finish(xs, yconv_s, o1, o2, lam_params, g, mod_s, w_out_b, ln1, w_gate_b, w_up_b, w_down_b,
                            ln2, alpha, lam_init)
        ks_l.append(k_s.reshape(bs, t_len, N_HEADS, HEAD_W))
        vs_l.append(v_s.reshape(bs, t_len, N_HEADS, V_HEAD))
        cs_l.append(u_s.reshape(bs, t_len, cc)[:, t_len - (CONV_K - 1):])

    return (xp, xs.reshape(bs, t_len, d), jnp.stack(kp_l), jnp.stack(vp_l), jnp.stack(cp_l),
            jnp.stack(ks_l), jnp.stack(vs_l), jnp.stack(cs_l))
```

```python
import functools
import math

import jax
import jax.numpy as jnp
import numpy as np
from jax import lax
from jax.experimental import pallas as pl
from jax.experimental.pallas import tpu as pltpu

F32 = jnp.float32
BF16 = jnp.bfloat16

N_HEADS = 4
QK_HEAD = 64
V_HEAD = 128
HEAD_W = 2 * QK_HEAD
CONV_K = 3
ROT_DIM = QK_HEAD // 4
ROPE_THETA = 500000.0
N_MOD = 6
LN_EPS = 1e-5
LANES = 128
SUBLANES = 8
VMEM_LIMIT_BYTES = 56 * 1024 * 1024
NEG_INF = float("-inf")
Q_SCALE = QK_HEAD ** -0.5 * math.log2(math.e)


def _params(n_axes):
    return pltpu.CompilerParams(dimension_semantics=("arbitrary",) * n_axes,
                                vmem_limit_bytes=VMEM_LIMIT_BYTES)


def _resident(shape):
    return pl.BlockSpec(shape, lambda *_: (0,) * len(shape), pipeline_mode=pl.Buffered(1))


def _largest_divisor(n, cap):
    return max(d for d in range(1, min(n, cap) + 1) if n % d == 0)


def _layer_norm(x, g, b):
    mu = jnp.mean(x, axis=-1, keepdims=True)
    xc = x - mu
    var = jnp.mean(xc * xc, axis=-1, keepdims=True)
    return xc * lax.rsqrt(var + LN_EPS) * g + b


def _adaln_kernel(c_ref, w_ref, b_ref, o_ref):
    a = jax.nn.silu(c_ref[...]).astype(BF16)
    o_ref[...] = jnp.dot(a, w_ref[...].astype(BF16), preferred_element_type=F32) + b_ref[...]


def _adaln(c, w, b):
    n, d = c.shape
    n_out = w.shape[1]
    tn = 1536 if n_out % 1536 == 0 else n_out
    return pl.pallas_call(
        _adaln_kernel,
        grid=(n_out // tn,),
        in_specs=[pl.BlockSpec((n, d), lambda j: (0, 0)),
                  pl.BlockSpec((d, tn), lambda j: (0, j)),
                  pl.BlockSpec((1, tn), lambda j: (0, j))],
        out_specs=pl.BlockSpec((n, tn), lambda j: (0, j)),
        out_shape=jax.ShapeDtypeStruct((n, n_out), F32),
        compiler_params=_params(1),
        name="adaln",
    )(c, w, b.reshape(1, n_out))


def _rope_tables(pos):
    inv = ROPE_THETA ** (-np.arange(0, ROT_DIM, 2, dtype=np.float64) / ROT_DIM)
    ang = np.asarray(pos, np.float64)[:, None] * inv[None, :]
    cos, sin = np.cos(ang), np.sin(ang)
    t, half = ang.shape[0], ROT_DIM // 2
    c = np.concatenate([cos, cos, np.ones((t, QK_HEAD - ROT_DIM))], axis=1)
    s_up = np.concatenate([-sin, np.zeros((t, QK_HEAD - half))], axis=1)
    s_dn = np.concatenate([np.zeros((t, half)), sin, np.zeros((t, QK_HEAD - ROT_DIM))], axis=1)
    return np.stack([np.tile(a, (1, LANES // QK_HEAD)) for a in (c, s_up, s_dn)]).astype(np.float32)


def _rope(x, tab_ref):
    c, s_up, s_dn = tab_ref[0], tab_ref[1], tab_ref[2]
    half = ROT_DIM // 2
    cols = []
    for i in range(x.shape[1] // LANES):
        xc = x[:, i * LANES:(i + 1) * LANES]
        cols.append(xc * c + pltpu.roll(xc, LANES - half, 1) * s_up + pltpu.roll(xc, half, 1) * s_dn)
    return jnp.concatenate(cols, axis=1)


def _prompt_in_kernel(x_ref, mod_ref, w_ref, cw_ref, tab_ref,
                      yconv_ref, q_ref, kb_ref, vt_ref, k_ref, v_ref, cst_ref, carry_ref, ubuf_ref, *, ts, cc):
    s, b = pl.program_id(0), pl.program_id(1)

    @pl.when(s == 0)
    def _():
        carry_ref[b] = jnp.zeros((SUBLANES, cc), F32)

    mod = mod_ref[0]
    xm = (x_ref[0] * (1.0 + mod[1:2]) + mod[0:1]).astype(BF16)

    z = jnp.dot(xm, w_ref[:, 0:3 * cc], preferred_element_type=F32)
    h, gb, gc = z[:, 0:cc], z[:, cc:2 * cc], z[:, 2 * cc:3 * cc]
    u = gc * h

    ubuf_ref[0:SUBLANES] = carry_ref[b]
    ubuf_ref[SUBLANES:SUBLANES + ts] = u
    u1 = ubuf_ref[SUBLANES - 1:SUBLANES - 1 + ts]
    u2 = ubuf_ref[SUBLANES - 2:SUBLANES - 2 + ts]
    cw = cw_ref[...]
    yconv_ref[0] = (gb * (cw[0:1] * u2 + cw[1:2] * u1 + cw[2:3] * u)).astype(BF16)
    carry_ref[b] = u[ts - SUBLANES:]
    cst_ref[0, 0] = u[ts - (CONV_K - 1):]

    z = jnp.dot(xm, w_ref[:, 3 * cc:], preferred_element_type=F32)
    hw = N_HEADS * HEAD_W
    q_ref[0] = (_rope(z[:, 0:hw], tab_ref) * Q_SCALE).astype(BF16)
    k = _rope(z[:, hw:2 * hw], tab_ref)
    v = z[:, 2 * hw:]
    kb_ref[0] = k.astype(BF16)
    vt_ref[0] = v.T.astype(BF16)
    for h in range(N_HEADS):
        rows = pl.ds(h, ts, stride=N_HEADS)
        k_ref[0, rows, :] = k[:, h * HEAD_W:(h + 1) * HEAD_W]
        v_ref[0, rows, :] = v[:, h * V_HEAD:(h + 1) * V_HEAD]


def _prompt_in(x, mod, w_in, conv_w, tab, ts):
    bsz, seq, d = x.shape
    cc = conv_w.shape[1]
    hw = N_HEADS * HEAD_W
    n_in = w_in.shape[1]
    tok = lambda s, b: (b, s, 0)
    kern = functools.partial(_prompt_in_kernel, ts=ts, cc=cc)
    return pl.pallas_call(
        kern,
        grid=(seq // ts, bsz),
        in_specs=[pl.BlockSpec((1, ts, d), tok),
                  pl.BlockSpec((1, N_MOD, d), lambda s, b: (b, 0, 0)),
                  _resident((d, n_in)),
                  _resident((CONV_K, cc)),
                  pl.BlockSpec((3, ts, LANES), lambda s, b: (0, s, 0))],
        out_specs=[pl.BlockSpec((1, ts, cc), tok),
                   pl.BlockSpec((1, ts, hw), tok),
                   pl.BlockSpec((1, ts, hw), tok),
                   pl.BlockSpec((1, N_HEADS * V_HEAD, ts), lambda s, b: (b, 0, s)),
                   pl.BlockSpec((1, ts * N_HEADS, HEAD_W), tok),
                   pl.BlockSpec((1, ts * N_HEADS, V_HEAD), tok),
                   pl.BlockSpec((1, 1, CONV_K - 1, cc), lambda s, b: (s, b, 0, 0))],
        out_shape=[jax.ShapeDtypeStruct((bsz, seq, cc), BF16),
                   jax.ShapeDtypeStruct((bsz, seq, hw), BF16),
                   jax.ShapeDtypeStruct((bsz, seq, hw), BF16),
                   jax.ShapeDtypeStruct((bsz, N_HEADS * V_HEAD, seq), BF16),
                   jax.ShapeDtypeStruct((bsz, seq * N_HEADS, HEAD_W), F32),
                   jax.ShapeDtypeStruct((bsz, seq * N_HEADS, V_HEAD), F32),
                   jax.ShapeDtypeStruct((seq // ts, bsz, CONV_K - 1, cc), F32)],
        scratch_shapes=[pltpu.VMEM((bsz, SUBLANES, cc), F32),
                        pltpu.VMEM((ts + SUBLANES, cc), F32)],
        compiler_params=_params(2),
        name="prompt_in",
    )(x, mod, w_in, conv_w, tab)


def _lambda_value(lam_ref, lam_init):
    lp = lam_ref[...]
    a = jnp.sum(lp[0:1] * lp[1:2], axis=1, keepdims=True)
    b = jnp.sum(lp[2:3] * lp[3:4], axis=1, keepdims=True)
    return jnp.exp(a) - jnp.exp(b) + lam_init


def _head_out(o1, o2, lam, g, lam_init):
    d = o1 - lam * o2
    d = d * lax.rsqrt(jnp.mean(d * d, axis=-1, keepdims=True) + LN_EPS) * g
    return d * (1.0 - lam_init)


def _softmax_step(carry, s, v):
    m, l, acc = carry
    m_new = jnp.maximum(m, jnp.max(s, axis=1, keepdims=True))
    alpha = jnp.exp2(m - m_new)
    p = jnp.exp2(s - m_new)
    l = alpha * l + jnp.sum(p, axis=1, keepdims=True)
    acc = alpha * acc + jnp.dot(p.astype(BF16), v, preferred_element_type=F32)
    return m_new, l, acc


def _prompt_attn_kernel(q_ref, k_ref, vt_ref, lam_ref, g_ref, o_ref, *, seq, tq, lam_init):
    lam = _lambda_value(lam_ref, lam_init)
    lane = lax.broadcasted_iota(jnp.int32, (tq, HEAD_W), 1)
    nt = (((1,), (1,)), ((), ()))
    g_col = g_ref[...]

    def stacked_queries(qi):
        q = q_ref[0, qi * tq:(qi + 1) * tq, :]
        zero = jnp.zeros_like(q)
        return jnp.concatenate([jnp.where(lane < QK_HEAD, q, zero), jnp.where(lane >= QK_HEAD, q, zero)], axis=0)

    def step(carry, qq, qi, ki):
        m, l, acc = carry
        keys = slice(ki * tq, (ki + 1) * tq)
        s = lax.dot_general(k_ref[0, keys, :], qq, nt, preferred_element_type=F32)
        if ki == qi:
            key = lax.broadcasted_iota(jnp.int32, s.shape, 0)
            col = lax.broadcasted_iota(jnp.int32, s.shape, 1)
            s = jnp.where(key <= jnp.where(col >= tq, col - tq, col), s, NEG_INF)
        m_new = jnp.maximum(m, jnp.max(s, axis=0, keepdims=True))
        alpha = jnp.exp2(m - m_new)
        p = jnp.exp2(s - m_new)
        l = alpha * l + jnp.sum(p, axis=0, keepdims=True)
        acc = alpha * acc + jnp.dot(vt_ref[0, :, keys], p.astype(BF16), preferred_element_type=F32)
        return m_new, l, acc

    for qi in range(seq // tq):
        qq = stacked_queries(qi)
        carry = (jnp.full((1, 2 * tq), NEG_INF, F32), jnp.zeros((1, 2 * tq), F32), jnp.zeros((V_HEAD, 2 * tq), F32))
        for ki in range(qi + 1):
            carry = step(carry, qq, qi, ki)
        _, l, acc = carry
        o = acc / l
        d = o[:, :tq] - lam * o[:, tq:]
        d = d * lax.rsqrt(jnp.mean(d * d, axis=0, keepdims=True) + LN_EPS) * g_col * (1.0 - lam_init)
        o_ref[0, qi * tq:(qi + 1) * tq, :] = d.T.astype(BF16)


def _prompt_attn(q, k, vt, lam_params, g, lam_init, tq):
    bsz, seq, _ = q.shape
    blk = pl.BlockSpec((1, seq, HEAD_W), lambda b, h: (b, 0, h))
    kern = functools.partial(_prompt_attn_kernel, seq=seq, tq=tq, lam_init=lam_init)
    return pl.pallas_call(
        kern,
        grid=(bsz, N_HEADS),
        in_specs=[blk, blk,
                  pl.BlockSpec((1, V_HEAD, seq), lambda b, h: (b, h, 0)),
                  pl.BlockSpec((4, QK_HEAD), lambda b, h: (0, 0)),
                  pl.BlockSpec((V_HEAD, 1), lambda b, h: (0, 0))],
        out_specs=blk,
        out_shape=jax.ShapeDtypeStruct((bsz, seq, N_HEADS * V_HEAD), BF16),
        compiler_params=_params(2),
        name="prompt_attn",
    )(q, k, vt, lam_params, g.reshape(V_HEAD, 1))


FINISH_PHASES = 4
SAMPLE_CHUNK_PAGES = 16
SAMPLE_RING_SLOTS = 2


def _finish_math(x, yc, ya, gate1, shift2, scale2, gate2, wo_ref, ln1g, ln1b, wg_ref, wu_ref, wd_ref,
                 ln2g, ln2b, alpha, around_phase=None):
    run = around_phase or (lambda i, phase: phase())
    cc = yc.shape[1]
    v = {}

    def project():
        y = (jnp.dot(yc, wo_ref[0:cc], preferred_element_type=F32)
             + jnp.dot(ya, wo_ref[cc:], preferred_element_type=F32))
        v["x1"] = _layer_norm(alpha * x + gate1 * y, ln1g, ln1b)
        v["hf"] = (v["x1"] * (1.0 + scale2) + shift2).astype(BF16)

    def gate():
        v["g"] = jnp.dot(v["hf"], wg_ref[...], preferred_element_type=F32)

    def up():
        u = jnp.dot(v["hf"], wu_ref[...], preferred_element_type=F32)
        v["act"] = (jax.nn.silu(v["g"]) * u).astype(BF16)

    def down():
        f = jnp.dot(v["act"], wd_ref[...], preferred_element_type=F32)
        v["out"] = _layer_norm(alpha * v["x1"] + gate2 * f, ln2g, ln2b)

    for i, phase in enumerate((project, gate, up, down)):
        run(i, phase)
    return v["out"]


def _finish_attn_kernel(pt_ref, x_ref, yc_ref, ya_ref, mod_ref, wo_ref, ln1_ref, wg_ref, wu_ref, wd_ref, ln2_ref,
                        wq_ref, kn_ref, vn_ref, ck_ref, cv_ref, o_ref, so_ref,
                        kbuf, vbuf, sem, m_ref, l_ref, acc_ref,
                        *, alpha, n_pages, pages_per_chunk, chunks_per_step, page_base, t_len):
    t = pl.program_id(0) * pl.num_programs(1) + pl.program_id(1)
    n_chunks = n_pages // pages_per_chunk
    total_chunks = pl.num_programs(0) * pl.num_programs(1) * chunks_per_step
    n_slots = kbuf.shape[0]
    page_rows = ck_ref.shape[1]
    nt = (((1,), (1,)), ((), ()))
    rows = wq_ref.shape[1]

    def copies(index):
        batch, chunk, slot = index // n_chunks, index % n_chunks, index % n_slots
        out = []
        for g in range(pages_per_chunk):
            pg = pt_ref[batch, chunk * pages_per_chunk + g] + page_base
            dst = pl.ds(g * page_rows, page_rows)
            out.append((pltpu.make_async_copy(ck_ref.at[pg], kbuf.at[slot, dst], sem.at[0, slot]),
                        pltpu.make_async_copy(cv_ref.at[pg], vbuf.at[slot, dst], sem.at[1, slot])))
        return out

    def start(index):
        for k_copy, v_copy in copies(index):
            k_copy.start(priority=0)
            v_copy.start(priority=1)

    def reset_state():
        m_ref[...] = jnp.full(m_ref.shape, NEG_INF, F32)
        l_ref[...] = jnp.zeros(l_ref.shape, F32)
        acc_ref[...] = jnp.zeros(acc_ref.shape, F32)

    @pl.when(t == 0)
    def _():
        reset_state()
        for index in range(n_slots):
            start(index)

    row_head = lax.broadcasted_iota(jnp.int32, (rows, page_rows), 0) // (2 * t_len)
    col_head = lax.broadcasted_iota(jnp.int32, (rows, page_rows), 1) % N_HEADS
    own_head = row_head == col_head

    def attend(s, v):
        m, l, acc = _softmax_step((m_ref[...], l_ref[...], acc_ref[...]), s, v)
        m_ref[...] = m
        l_ref[...] = l
        acc_ref[...] = acc

    def consume(index):
        slot = index % n_slots
        chunk_copies = copies(index)
        for k_copy, _ in chunk_copies:
            k_copy.wait()
        for _, v_copy in chunk_copies:
            v_copy.wait()
        head_bias = jnp.where(own_head, 0.0, NEG_INF)
        chunk_bias = jnp.concatenate([head_bias] * pages_per_chunk, axis=1)
        s = lax.dot_general(wq_ref[0], kbuf[slot].astype(BF16), nt, preferred_element_type=F32) + chunk_bias
        attend(s, vbuf[slot].astype(BF16))

    def refill(index):
        @pl.when(index + n_slots < total_chunks)
        def _():
            start(index + n_slots)

    def around_phase(i, matmul_phase):
        for k in range(chunks_per_step):
            if (0 if 2 * k < chunks_per_step else FINISH_PHASES // 2) == i:
                consume(t * chunks_per_step + k)
                refill(t * chunks_per_step + k)
        matmul_phase()

    mod = mod_ref[0]
    o_ref[0] = _finish_math(x_ref[0], yc_ref[0], ya_ref[0], mod[2:3], mod[3:4], mod[4:5], mod[5:6],
                            wo_ref, ln1_ref[0:1], ln1_ref[1:2], wg_ref, wu_ref, wd_ref,
                            ln2_ref[0:1], ln2_ref[1:2], alpha, around_phase)

    @pl.when(((t + 1) * chunks_per_step) % n_chunks == 0)
    def _():
        s = lax.dot_general(wq_ref[0], kn_ref[0].astype(BF16), nt, preferred_element_type=F32)
        new_rows = kn_ref.shape[1]
        q_step = lax.broadcasted_iota(jnp.int32, (rows, new_rows), 0) % t_len
        k_step = lax.broadcasted_iota(jnp.int32, (rows, new_rows), 1) // N_HEADS
        s = jnp.where(own_head[:, :new_rows] & (k_step <= q_step), s, NEG_INF)
        attend(s, vn_ref[0].astype(BF16))
        so_ref[0] = acc_ref[...] / l_ref[...]
        reset_state()


def _sample_finish_kernel(x_ref, yc_ref, o1_ref, o2_ref, lam_ref, g_ref, mod_ref, wo_ref, ln1_ref,
                          wg_ref, wu_ref, wd_ref, ln2_ref, o_ref, *, alpha, lam_init):
    lam = _lambda_value(lam_ref, lam_init)
    o1, o2 = o1_ref[...], o2_ref[...]
    ya = jnp.concatenate(
        [_head_out(o1[:, h * V_HEAD:(h + 1) * V_HEAD], o2[:, h * V_HEAD:(h + 1) * V_HEAD], lam, g_ref[...], lam_init)
         for h in range(N_HEADS)], axis=1).astype(BF16)
    d = x_ref.shape[1]
    gate1, shift2, scale2, gate2 = (mod_ref[:, i * d:(i + 1) * d] for i in range(2, N_MOD))
    o_ref[...] = _finish_math(x_ref[...], yc_ref[...], ya, gate1, shift2, scale2, gate2,
                              wo_ref, ln1_ref[0:1], ln1_ref[1:2], wg_ref, wu_ref, wd_ref,
                              ln2_ref[0:1], ln2_ref[1:2], alpha)


def _finish_attn(x, yc, ya, mod, w_out, ln1, w_gate, w_up, w_down, ln2, alpha, tm,
                 page_table, wq, k_new, v_new, cache_k, cache_v, page_base, t_len):
    bsz, seq, d = x.shape
    cc, d_ff = yc.shape[2], w_gate.shape[1]
    n_s = seq // tm
    sample_bsz, n_pages = page_table.shape
    _, page_rows, width = cache_k.shape
    rows = wq.shape[1]
    pages_per_step, ragged = divmod(sample_bsz * n_pages, bsz * n_s)
    assert not ragged and pages_per_step > 0
    g = _largest_divisor(math.gcd(pages_per_step, n_pages), SAMPLE_CHUNK_PAGES)
    chunks_per_step, n_chunks = pages_per_step // g, n_pages // g
    n_slots = SAMPLE_RING_SLOTS
    assert bsz * n_s * chunks_per_step >= n_slots
    kern = functools.partial(_finish_attn_kernel, alpha=alpha, n_pages=n_pages, pages_per_chunk=g,
                             chunks_per_step=chunks_per_step, page_base=page_base, t_len=t_len)
    tok = lambda b, s, pt: (b, s, 0)
    fixed = lambda shape: pl.BlockSpec(shape, lambda b, s, pt: (0,) * len(shape), pipeline_mode=pl.Buffered(1))
    sample = lambda r: pl.BlockSpec(
        (1, r, width), lambda b, s, pt: (((b * n_s + s) * chunks_per_step) // n_chunks, 0, 0))
    grid_spec = pltpu.PrefetchScalarGridSpec(
        num_scalar_prefetch=1,
        grid=(bsz, n_s),
        in_specs=[pl.BlockSpec((1, tm, d), tok),
                  pl.BlockSpec((1, tm, cc), tok),
                  pl.BlockSpec((1, tm, d - cc), tok),
                  pl.BlockSpec((1, N_MOD, d), lambda b, s, pt: (b, 0, 0)),
                  fixed((d, d)), fixed((2, d)),
                  fixed((d, d_ff)), fixed((d, d_ff)), fixed((d_ff, d)), fixed((2, d)),
                  sample(rows), sample(k_new.shape[1]), sample(v_new.shape[1]),
                  pl.BlockSpec(memory_space=pl.ANY),
                  pl.BlockSpec(memory_space=pl.ANY)],
        out_specs=[pl.BlockSpec((1, tm, d), tok), sample(rows)],
        scratch_shapes=[pltpu.VMEM((n_slots, g * page_rows, width), F32),
                        pltpu.VMEM((n_slots, g * page_rows, width), F32),
                        pltpu.SemaphoreType.DMA((2, n_slots)),
                        pltpu.VMEM((rows, 1), F32), pltpu.VMEM((rows, 1), F32), pltpu.VMEM((rows, V_HEAD), F32)],
    )
    return pl.pallas_call(
        kern,
        grid_spec=grid_spec,
        out_shape=[jax.ShapeDtypeStruct((bsz, seq, d), F32),
                   jax.ShapeDtypeStruct((sample_bsz, rows, width), F32)],
        compiler_params=_params(2),
        name="finish_attn",
    )(page_table, x, yc, ya, mod, w_out, ln1, w_gate, w_up, w_down, ln2, wq, k_new, v_new, cache_k, cache_v)


def _sample_finish(x, yc, o1, o2, lam_params, g, mod_rows, w_out, ln1, w_gate, w_up, w_down, ln2, alpha, lam_init):
    n, d = x.shape
    kern = functools.partial(_sample_finish_kernel, alpha=alpha, lam_init=lam_init)
    args = (x, yc, o1, o2, lam_params, g, mod_rows, w_out, ln1, w_gate, w_up, w_down, ln2)
    return pl.pallas_call(
        kern,
        grid=(1,),
        in_specs=[_resident(a.shape) for a in args],
        out_specs=pl.BlockSpec((n, d), lambda i: (0, 0)),
        out_shape=jax.ShapeDtypeStruct((n, d), F32),
        compiler_params=_params(1),
        name="sample_finish",
    )(*args)


def _sample_in_kernel(x_ref, mod_ref, w_ref, cw_ref, st1_ref, st2_ref, tab_ref,
                      yconv_ref, q_ref, k_ref, v_ref, u_ref, *, t_len, cc):
    d = x_ref.shape[1]
    xm = (x_ref[...] * (1.0 + mod_ref[:, d:2 * d]) + mod_ref[:, 0:d]).astype(BF16)
    z = jnp.dot(xm, w_ref[...], preferred_element_type=F32)
    h, gb, gc = z[:, 0:cc], z[:, cc:2 * cc], z[:, 2 * cc:3 * cc]
    u = gc * h
    n = u.shape[0]
    t = lax.broadcasted_iota(jnp.int32, (n, cc), 0) % t_len
    u1 = jnp.where(t >= 1, pltpu.roll(u, 1, 0), st1_ref[...])
    u2 = jnp.where(t >= 2, pltpu.roll(u, 2, 0), st2_ref[...])
    cw = cw_ref[...]
    yconv_ref[...] = (gb * (cw[0:1] * u2 + cw[1:2] * u1 + cw[2:3] * u)).astype(BF16)
    u_ref[...] = u
    hw = N_HEADS * HEAD_W
    q_ref[...] = (_rope(z[:, 3 * cc:3 * cc + hw], tab_ref) * Q_SCALE).astype(BF16)
    k_ref[...] = _rope(z[:, 3 * cc + hw:3 * cc + 2 * hw], tab_ref)
    v_ref[...] = z[:, 3 * cc + 2 * hw:]


def _sample_in(x, mod_rows, w_in, conv_w, st1, st2, tab, t_len):
    n, _ = x.shape
    cc = conv_w.shape[1]
    hw = N_HEADS * HEAD_W
    kern = functools.partial(_sample_in_kernel, t_len=t_len, cc=cc)
    args = (x, mod_rows, w_in, conv_w, st1, st2, tab)
    out_shapes = [jax.ShapeDtypeStruct((n, cc), BF16), jax.ShapeDtypeStruct((n, hw), BF16),
                  jax.ShapeDtypeStruct((n, hw), F32), jax.ShapeDtypeStruct((n, N_HEADS * V_HEAD), F32),
                  jax.ShapeDtypeStruct((n, cc), F32)]
    return pl.pallas_call(
        kern,
        grid=(1,),
        in_specs=[_resident(a.shape) for a in args],
        out_specs=[pl.BlockSpec(o.shape, lambda i: (0, 0)) for o in out_shapes],
        out_shape=out_shapes,
        compiler_params=_params(1),
        name="sample_in",
    )(*args)


def _query_rows(q, bsz, t_len):
    q5 = q.reshape(bsz, t_len, N_HEADS, 2, QK_HEAD).transpose(0, 2, 3, 1, 4)
    eye_c = jnp.eye(2, dtype=q.dtype)
    w = q5[:, :, :, :, None, :] * eye_c[None, None, :, None, :, None]
    return w.reshape(bsz, N_HEADS * 2 * t_len, HEAD_W)


def _token_head_rows(a, bsz, t_len, rows):
    a = a.reshape(bsz, t_len * N_HEADS, a.shape[1] // N_HEADS)
    return jnp.pad(a, ((0, 0), (0, rows - t_len * N_HEADS), (0, 0)))


def _component_outputs(o, bsz, t_len):
    o5 = o.reshape(bsz, N_HEADS, 2, t_len, V_HEAD).transpose(2, 0, 3, 1, 4)
    o5 = o5.reshape(2, bsz * t_len, N_HEADS * V_HEAD)
    return o5[0], o5[1]


def kernel(x_prompt, x_sample, c_prompt, c_sample, cache_k, cache_v, state_conv, page_table, w_ada, b_ada, w_in, conv_w, lambda_q1, lambda_k1, lambda_q2, lambda_k2, subln_g, w_out, ln1_g, ln1_b, w_gate, w_up, w_down, ln2_g, ln2_b):
    bp, s_len, d = x_prompt.shape
    bs, t_len, _ = x_sample.shape
    depth, n_pool, page = cache_k.shape[:3]
    past = page_table.shape[1] * page
    cc = conv_w.shape[2]
    alpha = (2.0 * depth) ** 0.25
    assert (bs * t_len) % SUBLANES == 0 and t_len <= page

    ts = min(1024, s_len)
    tq = min(512, s_len)
    tm = min(512, s_len)

    tab_p = jnp.asarray(_rope_tables(np.arange(s_len)))
    tab_s = jnp.asarray(np.tile(_rope_tables(past + np.arange(t_len)), (1, bs, 1)))
    ck = cache_k.reshape(depth * n_pool, page * N_HEADS, HEAD_W)
    cv = cache_v.reshape(depth * n_pool, page * N_HEADS, V_HEAD)
    c_all = jnp.concatenate([c_prompt, jnp.repeat(c_sample, t_len, axis=0)], axis=0)

    xp, xs = x_prompt, x_sample.reshape(bs * t_len, d)
    kp_l, vp_l, cp_l, ks_l, vs_l, cs_l = [], [], [], [], [], []
    for l in range(depth):
        lam_init = 0.8 - 0.6 * math.exp(-0.3 * l)
        w_in_b, w_out_b = w_in[l].astype(BF16), w_out[l].astype(BF16)
        w_gate_b, w_up_b, w_down_b = w_gate[l].astype(BF16), w_up[l].astype(BF16), w_down[l].astype(BF16)
        lam_params = jnp.stack([lambda_q1[l], lambda_k1[l], lambda_q2[l], lambda_k2[l]])
        g = subln_g[l].reshape(1, V_HEAD)
        ln1 = jnp.stack([ln1_g[l], ln1_b[l]])
        ln2 = jnp.stack([ln2_g[l], ln2_b[l]])

        mod = _adaln(c_all, w_ada[l], b_ada[l])
        mod_p = mod[:bp].reshape(bp, N_MOD, d)
        mod_s = mod[bp:]

        yconv, q, kb, vt, k, v, cst = _prompt_in(xp, mod_p, w_in_b, conv_w[l], tab_p, ts)
        yattn = _prompt_attn(q, kb, vt, lam_params, g, lam_init, tq)
        kp_l.append(k.reshape(bp, s_len, N_HEADS, HEAD_W))
        vp_l.append(v.reshape(bp, s_len, N_HEADS, V_HEAD))
        cp_l.append(cst[-1])

        st = state_conv[l]
        zero = jnp.zeros((bs, 1, cc), F32)
        st1 = jnp.concatenate([st[:, 1:2]] + [zero] * (t_len - 1), axis=1).reshape(bs * t_len, cc)
        st2 = jnp.concatenate([st[:, 0:1], st[:, 1:2]] + [zero] * (t_len - 2), axis=1).reshape(bs * t_len, cc)
        yconv_s, q_s, k_s, v_s, u_s = _sample_in(xs, mod_s, w_in_b, conv_w[l], st1, st2, tab_s, t_len)
        wq = _query_rows(q_s, bs, t_len)
        rows = pl.cdiv(t_len * N_HEADS, LANES) * LANES

        xp, o = _finish_attn(xp, yconv, yattn, mod_p, w_out_b, ln1, w_gate_b, w_up_b, w_down_b, ln2, alpha, tm,
                             page_table, wq, _token_head_rows(k_s, bs, t_len, rows),
                             _token_head_rows(v_s, bs, t_len, rows), ck, cv, l * n_pool, t_len)
        o1, o2 = _component_outputs(o, bs, t_len)
        xs = _sample_finish(xs, yconv_s, o1, o2, lam_params, g, mod_s, w_out_b, ln1, w_gate_b, w_up_b, w_down_b,
                            ln2, alpha, lam_init)
        ks_l.append(k_s.reshape(bs, t_len, N_HEADS, HEAD_W))
        vs_l.append(v_s.reshape(bs, t_len, N_HEADS, V_HEAD))
        cs_l.append(u_s.reshape(bs, t_len, cc)[:, t_len - (CONV_K - 1):])

    return (xp, xs.reshape(bs, t_len, d), jnp.stack(kp_l), jnp.stack(vp_l), jnp.stack(cp_l),
            jnp.stack(ks_l), jnp.stack(vs_l), jnp.stack(cs_l))
```

```python
import functools
import math

import jax
import jax.numpy as jnp
import numpy as np
from jax import lax
from jax.experimental import pallas as pl
from jax.experimental.pallas import tpu as pltpu

F32 = jnp.float32
BF16 = jnp.bfloat16

N_HEADS = 4
QK_HEAD = 64
V_HEAD = 128
HEAD_W = 2 * QK_HEAD
CONV_K = 3
ROT_DIM = QK_HEAD // 4
ROPE_THETA = 500000.0
N_MOD = 6
LN_EPS = 1e-5
LANES = 128
SUBLANES = 8
SUM_ROWS = 16
VMEM_LIMIT_BYTES = 56 * 1024 * 1024
NEG_INF = float("-inf")
Q_SCALE = QK_HEAD ** -0.5 * math.log2(math.e)


def _params(n_axes):
    return pltpu.CompilerParams(dimension_semantics=("arbitrary",) * n_axes,
                                vmem_limit_bytes=VMEM_LIMIT_BYTES)


def _resident(shape):
    return pl.BlockSpec(shape, lambda *_: (0,) * len(shape), pipeline_mode=pl.Buffered(1))


def _largest_divisor(n, cap):
    return max(d for d in range(1, min(n, cap) + 1) if n % d == 0)


def _layer_norm(x, g, b):
    mu = jnp.mean(x, axis=-1, keepdims=True)
    xc = x - mu
    var = jnp.mean(xc * xc, axis=-1, keepdims=True)
    return xc * lax.rsqrt(var + LN_EPS) * g + b


def _adaln_kernel(c_ref, w_ref, b_ref, o_ref):
    a = jax.nn.silu(c_ref[...]).astype(BF16)
    o_ref[...] = jnp.dot(a, w_ref[...].astype(BF16), preferred_element_type=F32) + b_ref[...]


def _adaln(c, w, b):
    n, d = c.shape
    n_out = w.shape[1]
    tn = 1536 if n_out % 1536 == 0 else n_out
    return pl.pallas_call(
        _adaln_kernel,
        grid=(n_out // tn,),
        in_specs=[pl.BlockSpec((n, d), lambda j: (0, 0)),
                  pl.BlockSpec((d, tn), lambda j: (0, j)),
                  pl.BlockSpec((1, tn), lambda j: (0, j))],
        out_specs=pl.BlockSpec((n, tn), lambda j: (0, j)),
        out_shape=jax.ShapeDtypeStruct((n, n_out), F32),
        compiler_params=_params(1),
        name="adaln",
    )(c, w, b.reshape(1, n_out))


def _rope_tables(pos):
    inv = ROPE_THETA ** (-np.arange(0, ROT_DIM, 2, dtype=np.float64) / ROT_DIM)
    ang = np.asarray(pos, np.float64)[:, None] * inv[None, :]
    cos, sin = np.cos(ang), np.sin(ang)
    t, half = ang.shape[0], ROT_DIM // 2
    c = np.concatenate([cos, cos, np.ones((t, QK_HEAD - ROT_DIM))], axis=1)
    s_up = np.concatenate([-sin, np.zeros((t, QK_HEAD - half))], axis=1)
    s_dn = np.concatenate([np.zeros((t, half)), sin, np.zeros((t, QK_HEAD - ROT_DIM))], axis=1)
    return np.stack([np.tile(a, (1, LANES // QK_HEAD)) for a in (c, s_up, s_dn)]).astype(np.float32)


def _rope(x, tab_ref):
    c, s_up, s_dn = tab_ref[0], tab_ref[1], tab_ref[2]
    half = ROT_DIM // 2
    cols = []
    for i in range(x.shape[1] // LANES):
        xc = x[:, i * LANES:(i + 1) * LANES]
        cols.append(xc * c + pltpu.roll(xc, LANES - half, 1) * s_up + pltpu.roll(xc, half, 1) * s_dn)
    return jnp.concatenate(cols, axis=1)


def _prompt_in_kernel(x_ref, mod_ref, w_ref, cw_ref, tab_ref,
                      yconv_ref, q_ref, kb_ref, vt_ref, k_ref, v_ref, cst_ref, carry_ref, ubuf_ref, *, ts, cc):
    s, b = pl.program_id(0), pl.program_id(1)

    @pl.when(s == 0)
    def _():
        carry_ref[b] = jnp.zeros((SUBLANES, cc), F32)

    mod = mod_ref[0]
    xm = (x_ref[0] * (1.0 + mod[1:2]) + mod[0:1]).astype(BF16)

    z = jnp.dot(xm, w_ref[:, 0:3 * cc], preferred_element_type=F32)
    h, gb, gc = z[:, 0:cc], z[:, cc:2 * cc], z[:, 2 * cc:3 * cc]
    u = gc * h

    ubuf_ref[0:SUBLANES] = carry_ref[b]
    ubuf_ref[SUBLANES:SUBLANES + ts] = u
    u1 = ubuf_ref[SUBLANES - 1:SUBLANES - 1 + ts]
    u2 = ubuf_ref[SUBLANES - 2:SUBLANES - 2 + ts]
    cw = cw_ref[...]
    yconv_ref[0] = (gb * (cw[0:1] * u2 + cw[1:2] * u1 + cw[2:3] * u)).astype(BF16)
    carry_ref[b] = u[ts - SUBLANES:]
    cst_ref[0, 0] = u[ts - (CONV_K - 1):]

    z = jnp.dot(xm, w_ref[:, 3 * cc:], preferred_element_type=F32)
    hw = N_HEADS * HEAD_W
    q_ref[0] = (_rope(z[:, 0:hw], tab_ref) * Q_SCALE).astype(BF16)
    k = _rope(z[:, hw:2 * hw], tab_ref)
    v = z[:, 2 * hw:]
    kb_ref[0] = k.astype(BF16)
    vt_ref[0] = v.T.astype(BF16)
    for h in range(N_HEADS):
        rows = pl.ds(h, ts, stride=N_HEADS)
        k_ref[0, rows, :] = k[:, h * HEAD_W:(h + 1) * HEAD_W]
        v_ref[0, rows, :] = v[:, h * V_HEAD:(h + 1) * V_HEAD]


def _prompt_in(x, mod, w_in, conv_w, tab, ts):
    bsz, seq, d = x.shape
    cc = conv_w.shape[1]
    hw = N_HEADS * HEAD_W
    n_in = w_in.shape[1]
    tok = lambda s, b: (b, s, 0)
    kern = functools.partial(_prompt_in_kernel, ts=ts, cc=cc)
    return pl.pallas_call(
        kern,
        grid=(seq // ts, bsz),
        in_specs=[pl.BlockSpec((1, ts, d), tok),
                  pl.BlockSpec((1, N_MOD, d), lambda s, b: (b, 0, 0)),
                  _resident((d, n_in)),
                  _resident((CONV_K, cc)),
                  pl.BlockSpec((3, ts, LANES), lambda s, b: (0, s, 0))],
        out_specs=[pl.BlockSpec((1, ts, cc), tok),
                   pl.BlockSpec((1, ts, hw), tok),
                   pl.BlockSpec((1, ts, hw), tok),
                   pl.BlockSpec((1, N_HEADS * V_HEAD, ts), lambda s, b: (b, 0, s)),
                   pl.BlockSpec((1, ts * N_HEADS, HEAD_W), tok),
                   pl.BlockSpec((1, ts * N_HEADS, V_HEAD), tok),
                   pl.BlockSpec((1, 1, CONV_K - 1, cc), lambda s, b: (s, b, 0, 0))],
        out_shape=[jax.ShapeDtypeStruct((bsz, seq, cc), BF16),
                   jax.ShapeDtypeStruct((bsz, seq, hw), BF16),
                   jax.ShapeDtypeStruct((bsz, seq, hw), BF16),
                   jax.ShapeDtypeStruct((bsz, N_HEADS * V_HEAD, seq), BF16),
                   jax.ShapeDtypeStruct((bsz, seq * N_HEADS, HEAD_W), F32),
                   jax.ShapeDtypeStruct((bsz, seq * N_HEADS, V_HEAD), F32),
                   jax.ShapeDtypeStruct((seq // ts, bsz, CONV_K - 1, cc), F32)],
        scratch_shapes=[pltpu.VMEM((bsz, SUBLANES, cc), F32),
                        pltpu.VMEM((ts + SUBLANES, cc), F32)],
        compiler_params=_params(2),
        name="prompt_in",
    )(x, mod, w_in, conv_w, tab)


def _lambda_value(lam_ref, lam_init):
    lp = lam_ref[...]
    a = jnp.sum(lp[0:1] * lp[1:2], axis=1, keepdims=True)
    b = jnp.sum(lp[2:3] * lp[3:4], axis=1, keepdims=True)
    return jnp.exp(a) - jnp.exp(b) + lam_init


def _head_out(o1, o2, lam, g, lam_init):
    d = o1 - lam * o2
    d = d * lax.rsqrt(jnp.mean(d * d, axis=-1, keepdims=True) + LN_EPS) * g
    return d * (1.0 - lam_init)


def _softmax_step(carry, s, v):
    m, l, acc = carry
    m_new = jnp.maximum(m, jnp.max(s, axis=1, keepdims=True))
    alpha = jnp.exp2(m - m_new)
    p = jnp.exp2(s - m_new)
    l = alpha * l + jnp.sum(p, axis=1, keepdims=True)
    acc = alpha * acc + jnp.dot(p.astype(BF16), v, preferred_element_type=F32)
    return m_new, l, acc


def _prompt_attn_kernel(q_ref, k_ref, vt_ref, lam_ref, g_ref, o_ref, *, seq, tq, lam_init):
    lam = _lambda_value(lam_ref, lam_init)
    lane = lax.broadcasted_iota(jnp.int32, (tq, HEAD_W), 1)
    nt = (((1,), (1,)), ((), ()))
    g_col = g_ref[...]

    def stacked_queries(qi):
        q = q_ref[0, qi * tq:(qi + 1) * tq, :]
        zero = jnp.zeros_like(q)
        return jnp.concatenate([jnp.where(lane < QK_HEAD, q, zero), jnp.where(lane >= QK_HEAD, q, zero)], axis=0)

    ones_rows = jnp.ones((SUM_ROWS, tq), BF16)

    def step(carry, qq, qi, ki):
        m, acc = carry
        keys = slice(ki * tq, (ki + 1) * tq)
        s = lax.dot_general(k_ref[0, keys, :], qq, nt, preferred_element_type=F32)
        if ki == qi:
            key = lax.broadcasted_iota(jnp.int32, s.shape, 0)
            col = lax.broadcasted_iota(jnp.int32, s.shape, 1)
            s = jnp.where(key <= jnp.where(col >= tq, col - tq, col), s, NEG_INF)
        m_new = jnp.maximum(m, jnp.max(s, axis=0, keepdims=True))
        p = jnp.exp2(s - m_new).astype(BF16)
        vt_ones = jnp.concatenate([vt_ref[0, :, keys], ones_rows], axis=0)
        acc = jnp.exp2(m - m_new) * acc + jnp.dot(vt_ones, p, preferred_element_type=F32)
        return m_new, acc

    for qi in range(seq // tq):
        qq = stacked_queries(qi)
        carry = (jnp.full((1, 2 * tq), NEG_INF, F32), jnp.zeros((V_HEAD + SUM_ROWS, 2 * tq), F32))
        for ki in range(qi + 1):
            carry = step(carry, qq, qi, ki)
        _, acc = carry
        o = acc[:V_HEAD] / acc[V_HEAD:V_HEAD + 1]
        d = o[:, :tq] - lam * o[:, tq:]
        d = d * lax.rsqrt(jnp.mean(d * d, axis=0, keepdims=True) + LN_EPS) * g_col * (1.0 - lam_init)
        o_ref[0, qi * tq:(qi + 1) * tq, :] = d.T.astype(BF16)


def _prompt_attn(q, k, vt, lam_params, g, lam_init, tq):
    bsz, seq, _ = q.shape
    blk = pl.BlockSpec((1, seq, HEAD_W), lambda b, h: (b, 0, h))
    kern = functools.partial(_prompt_attn_kernel, seq=seq, tq=tq, lam_init=lam_init)
    return pl.pallas_call(
        kern,
        grid=(bsz, N_HEADS),
        in_specs=[blk, blk,
                  pl.BlockSpec((1, V_HEAD, seq), lambda b, h: (b, h, 0)),
                  pl.BlockSpec((4, QK_HEAD), lambda b, h: (0, 0)),
                  pl.BlockSpec((V_HEAD, 1), lambda b, h: (0, 0))],
        out_specs=blk,
        out_shape=jax.ShapeDtypeStruct((bsz, seq, N_HEADS * V_HEAD), BF16),
        compiler_params=_params(2),
        name="prompt_attn",
    )(q, k, vt, lam_params, g.reshape(V_HEAD, 1))


FINISH_PHASES = 4
SAMPLE_CHUNK_PAGES = 16
SAMPLE_RING_SLOTS = 2


def _finish_math(x, yc, ya, gate1, shift2, scale2, gate2, wo_ref, ln1g, ln1b, wg_ref, wu_ref, wd_ref,
                 ln2g, ln2b, alpha, around_phase=None):
    run = around_phase or (lambda i, phase: phase())
    cc = yc.shape[1]
    v = {}

    def project():
        y = (jnp.dot(yc, wo_ref[0:cc], preferred_element_type=F32)
             + jnp.dot(ya, wo_ref[cc:], preferred_element_type=F32))
        v["x1"] = _layer_norm(alpha * x + gate1 * y, ln1g, ln1b)
        v["hf"] = (v["x1"] * (1.0 + scale2) + shift2).astype(BF16)

    def gate():
        v["g"] = jnp.dot(v["hf"], wg_ref[...], preferred_element_type=F32)

    def up():
        u = jnp.dot(v["hf"], wu_ref[...], preferred_element_type=F32)
        v["act"] = (jax.nn.silu(v["g"]) * u).astype(BF16)

    def down():
        f = jnp.dot(v["act"], wd_ref[...], preferred_element_type=F32)
        v["out"] = _layer_norm(alpha * v["x1"] + gate2 * f, ln2g, ln2b)

    for i, phase in enumerate((project, gate, up, down)):
        run(i, phase)
    return v["out"]


def _finish_attn_kernel(pt_ref, x_ref, yc_ref, ya_ref, mod_ref, wo_ref, ln1_ref, wg_ref, wu_ref, wd_ref, ln2_ref,
                        wq_ref, kn_ref, vn_ref, ck_ref, cv_ref, o_ref, so_ref,
                        kbuf, vbuf, sem, m_ref, l_ref, acc_ref,
                        *, alpha, n_pages, pages_per_chunk, chunks_per_step, page_base, t_len):
    t = pl.program_id(0) * pl.num_programs(1) + pl.program_id(1)
    n_chunks = n_pages // pages_per_chunk
    total_chunks = pl.num_programs(0) * pl.num_programs(1) * chunks_per_step
    n_slots = kbuf.shape[0]
    page_rows = ck_ref.shape[1]
    nt = (((1,), (1,)), ((), ()))
    rows = wq_ref.shape[1]

    def copies(index):
        batch, chunk, slot = index // n_chunks, index % n_chunks, index % n_slots
        out = []
        for g in range(pages_per_chunk):
            pg = pt_ref[batch, chunk * pages_per_chunk + g] + page_base
            dst = pl.ds(g * page_rows, page_rows)
            out.append((pltpu.make_async_copy(ck_ref.at[pg], kbuf.at[slot, dst], sem.at[0, slot]),
                        pltpu.make_async_copy(cv_ref.at[pg], vbuf.at[slot, dst], sem.at[1, slot])))
        return out

    def start(index):
        for k_copy, v_copy in copies(index):
            k_copy.start(priority=0)
            v_copy.start(priority=1)

    def reset_state():
        m_ref[...] = jnp.full(m_ref.shape, NEG_INF, F32)
        l_ref[...] = jnp.zeros(l_ref.shape, F32)
        acc_ref[...] = jnp.zeros(acc_ref.shape, F32)

    @pl.when(t == 0)
    def _():
        reset_state()
        for index in range(n_slots):
            start(index)

    row_head = lax.broadcasted_iota(jnp.int32, (rows, page_rows), 0) // (2 * t_len)
    col_head = lax.broadcasted_iota(jnp.int32, (rows, page_rows), 1) % N_HEADS
    own_head = row_head == col_head

    def attend(s, v):
        m, l, acc = _softmax_step((m_ref[...], l_ref[...], acc_ref[...]), s, v)
        m_ref[...] = m
        l_ref[...] = l
        acc_ref[...] = acc

    def consume(index):
        slot = index % n_slots
        chunk_copies = copies(index)
        for k_copy, _ in chunk_copies:
            k_copy.wait()
        for _, v_copy in chunk_copies:
            v_copy.wait()
        head_bias = jnp.where(own_head, 0.0, NEG_INF)
        chunk_bias = jnp.concatenate([head_bias] * pages_per_chunk, axis=1)
        s = lax.dot_general(wq_ref[0], kbuf[slot].astype(BF16), nt, preferred_element_type=F32) + chunk_bias
        attend(s, vbuf[slot].astype(BF16))

    def refill(index):
        @pl.when(index + n_slots < total_chunks)
        def _():
            start(index + n_slots)

    def around_phase(i, matmul_phase):
        for k in range(chunks_per_step):
            if (0 if 2 * k < chunks_per_step else FINISH_PHASES // 2) == i:
                consume(t * chunks_per_step + k)
                refill(t * chunks_per_step + k)
        matmul_phase()

    mod = mod_ref[0]
    o_ref[0] = _finish_math(x_ref[0], yc_ref[0], ya_ref[0], mod[2:3], mod[3:4], mod[4:5], mod[5:6],
                            wo_ref, ln1_ref[0:1], ln1_ref[1:2], wg_ref, wu_ref, wd_ref,
                            ln2_ref[0:1], ln2_ref[1:2], alpha, around_phase)

    @pl.when(((t + 1) * chunks_per_step) % n_chunks == 0)
    def _():
        s = lax.dot_general(wq_ref[0], kn_ref[0].astype(BF16), nt, preferred_element_type=F32)
        new_rows = kn_ref.shape[1]
        q_step = lax.broadcasted_iota(jnp.int32, (rows, new_rows), 0) % t_len
        k_step = lax.broadcasted_iota(jnp.int32, (rows, new_rows), 1) // N_HEADS
        s = jnp.where(own_head[:, :new_rows] & (k_step <= q_step), s, NEG_INF)
        attend(s, vn_ref[0].astype(BF16))
        so_ref[0] = acc_ref[...] / l_ref[...]
        reset_state()


def _sample_finish_kernel(x_ref, yc_ref, o1_ref, o2_ref, lam_ref, g_ref, mod_ref, wo_ref, ln1_ref,
                          wg_ref, wu_ref, wd_ref, ln2_ref, o_ref, *, alpha, lam_init):
    lam = _lambda_value(lam_ref, lam_init)
    o1, o2 = o1_ref[...], o2_ref[...]
    ya = jnp.concatenate(
        [_head_out(o1[:, h * V_HEAD:(h + 1) * V_HEAD], o2[:, h * V_HEAD:(h + 1) * V_HEAD], lam, g_ref[...], lam_init)
         for h in range(N_HEADS)], axis=1).astype(BF16)
    d = x_ref.shape[1]
    gate1, shift2, scale2, gate2 = (mod_ref[:, i * d:(i + 1) * d] for i in range(2, N_MOD))
    o_ref[...] = _finish_math(x_ref[...], yc_ref[...], ya, gate1, shift2, scale2, gate2,
                              wo_ref, ln1_ref[0:1], ln1_ref[1:2], wg_ref, wu_ref, wd_ref,
                              ln2_ref[0:1], ln2_ref[1:2], alpha)


def _finish_attn(x, yc, ya, mod, w_out, ln1, w_gate, w_up, w_down, ln2, alpha, tm,
                 page_table, wq, k_new, v_new, cache_k, cache_v, page_base, t_len):
    bsz, seq, d = x.shape
    cc, d_ff = yc.shape[2], w_gate.shape[1]
    n_s = seq // tm
    sample_bsz, n_pages = page_table.shape
    _, page_rows, width = cache_k.shape
    rows = wq.shape[1]
    pages_per_step, ragged = divmod(sample_bsz * n_pages, bsz * n_s)
    assert not ragged and pages_per_step > 0
    g = _largest_divisor(math.gcd(pages_per_step, n_pages), SAMPLE_CHUNK_PAGES)
    chunks_per_step, n_chunks = pages_per_step // g, n_pages // g
    n_slots = SAMPLE_RING_SLOTS
    assert bsz * n_s * chunks_per_step >= n_slots
    kern = functools.partial(_finish_attn_kernel, alpha=alpha, n_pages=n_pages, pages_per_chunk=g,
                             chunks_per_step=chunks_per_step, page_base=page_base, t_len=t_len)
    tok = lambda b, s, pt: (b, s, 0)
    fixed = lambda shape: pl.BlockSpec(shape, lambda b, s, pt: (0,) * len(shape), pipeline_mode=pl.Buffered(1))
    sample = lambda r: pl.BlockSpec(
        (1, r, width), lambda b, s, pt: (((b * n_s + s) * chunks_per_step) // n_chunks, 0, 0))
    grid_spec = pltpu.PrefetchScalarGridSpec(
        num_scalar_prefetch=1,
        grid=(bsz, n_s),
        in_specs=[pl.BlockSpec((1, tm, d), tok),
                  pl.BlockSpec((1, tm, cc), tok),
                  pl.BlockSpec((1, tm, d - cc), tok),
                  pl.BlockSpec((1, N_MOD, d), lambda b, s, pt: (b, 0, 0)),
                  fixed((d, d)), fixed((2, d)),
                  fixed((d, d_ff)), fixed((d, d_ff)), fixed((d_ff, d)), fixed((2, d)),
                  sample(rows), sample(k_new.shape[1]), sample(v_new.shape[1]),
                  pl.BlockSpec(memory_space=pl.ANY),
                  pl.BlockSpec(memory_space=pl.ANY)],
        out_specs=[pl.BlockSpec((1, tm, d), tok), sample(rows)],
        scratch_shapes=[pltpu.VMEM((n_slots, g * page_rows, width), F32),
                        pltpu.VMEM((n_slots, g * page_rows, width), F32),
                        pltpu.SemaphoreType.DMA((2, n_slots)),
                        pltpu.VMEM((rows, 1), F32), pltpu.VMEM((rows, 1), F32), pltpu.VMEM((rows, V_HEAD), F32)],
    )
    return pl.pallas_call(
        kern,
        grid_spec=grid_spec,
        out_shape=[jax.ShapeDtypeStruct((bsz, seq, d), F32),
                   jax.ShapeDtypeStruct((sample_bsz, rows, width), F32)],
        compiler_params=_params(2),
        name="finish_attn",
    )(page_table, x, yc, ya, mod, w_out, ln1, w_gate, w_up, w_down, ln2, wq, k_new, v_new, cache_k, cache_v)


def _sample_finish(x, yc, o1, o2, lam_params, g, mod_rows, w_out, ln1, w_gate, w_up, w_down, ln2, alpha, lam_init):
    n, d = x.shape
    kern = functools.partial(_sample_finish_kernel, alpha=alpha, lam_init=lam_init)
    args = (x, yc, o1, o2, lam_params, g, mod_rows, w_out, ln1, w_gate, w_up, w_down, ln2)
    return pl.pallas_call(
        kern,
        grid=(1,),
        in_specs=[_resident(a.shape) for a in args],
        out_specs=pl.BlockSpec((n, d), lambda i: (0, 0)),
        out_shape=jax.ShapeDtypeStruct((n, d), F32),
        compiler_params=_params(1),
        name="sample_finish",
    )(*args)


def _sample_in_kernel(x_ref, mod_ref, w_ref, cw_ref, st1_ref, st2_ref, tab_ref,
                      yconv_ref, q_ref, k_ref, v_ref, u_ref, *, t_len, cc):
    d = x_ref.shape[1]
    xm = (x_ref[...] * (1.0 + mod_ref[:, d:2 * d]) + mod_ref[:, 0:d]).astype(BF16)
    z = jnp.dot(xm, w_ref[...], preferred_element_type=F32)
    h, gb, gc = z[:, 0:cc], z[:, cc:2 * cc], z[:, 2 * cc:3 * cc]
    u = gc * h
    n = u.shape[0]
    t = lax.broadcasted_iota(jnp.int32, (n, cc), 0) % t_len
    u1 = jnp.where(t >= 1, pltpu.roll(u, 1, 0), st1_ref[...])
    u2 = jnp.where(t >= 2, pltpu.roll(u, 2, 0), st2_ref[...])
    cw = cw_ref[...]
    yconv_ref[...] = (gb * (cw[0:1] * u2 + cw[1:2] * u1 + cw[2:3] * u)).astype(BF16)
    u_ref[...] = u
    hw = N_HEADS * HEAD_W
    q_ref[...] = (_rope(z[:, 3 * cc:3 * cc + hw], tab_ref) * Q_SCALE).astype(BF16)
    k_ref[...] = _rope(z[:, 3 * cc + hw:3 * cc + 2 * hw], tab_ref)
    v_ref[...] = z[:, 3 * cc + 2 * hw:]


def _sample_in(x, mod_rows, w_in, conv_w, st1, st2, tab, t_len):
    n, _ = x.shape
    cc = conv_w.shape[1]
    hw = N_HEADS * HEAD_W
    kern = functools.partial(_sample_in_kernel, t_len=t_len, cc=cc)
    args = (x, mod_rows, w_in, conv_w, st1, st2, tab)
    out_shapes = [jax.ShapeDtypeStruct((n, cc), BF16), jax.ShapeDtypeStruct((n, hw), BF16),
                  jax.ShapeDtypeStruct((n, hw), F32), jax.ShapeDtypeStruct((n, N_HEADS * V_HEAD), F32),
                  jax.ShapeDtypeStruct((n, cc), F32)]
    return pl.pallas_call(
        kern,
        grid=(1,),
        in_specs=[_resident(a.shape) for a in args],
        out_specs=[pl.BlockSpec(o.shape, lambda i: (0, 0)) for o in out_shapes],
        out_shape=out_shapes,
        compiler_params=_params(1),
        name="sample_in",
    )(*args)


def _query_rows(q, bsz, t_len):
    q5 = q.reshape(bsz, t_len, N_HEADS, 2, QK_HEAD).transpose(0, 2, 3, 1, 4)
    eye_c = jnp.eye(2, dtype=q.dtype)
    w = q5[:, :, :, :, None, :] * eye_c[None, None, :, None, :, None]
    return w.reshape(bsz, N_HEADS * 2 * t_len, HEAD_W)


def _token_head_rows(a, bsz, t_len, rows):
    a = a.reshape(bsz, t_len * N_HEADS, a.shape[1] // N_HEADS)
    return jnp.pad(a, ((0, 0), (0, rows - t_len * N_HEADS), (0, 0)))


def _component_outputs(o, bsz, t_len):
    o5 = o.reshape(bsz, N_HEADS, 2, t_len, V_HEAD).transpose(2, 0, 3, 1, 4)
    o5 = o5.reshape(2, bsz * t_len, N_HEADS * V_HEAD)
    return o5[0], o5[1]


def kernel(x_prompt, x_sample, c_prompt, c_sample, cache_k, cache_v, state_conv, page_table, w_ada, b_ada, w_in, conv_w, lambda_q1, lambda_k1, lambda_q2, lambda_k2, subln_g, w_out, ln1_g, ln1_b, w_gate, w_up, w_down, ln2_g, ln2_b):
    bp, s_len, d = x_prompt.shape
    bs, t_len, _ = x_sample.shape
    depth, n_pool, page = cache_k.shape[:3]
    past = page_table.shape[1] * page
    cc = conv_w.shape[2]
    alpha = (2.0 * depth) ** 0.25
    assert (bs * t_len) % SUBLANES == 0 and t_len <= page

    ts = min(1024, s_len)
    tq = min(512, s_len)
    tm = min(512, s_len)

    tab_p = jnp.asarray(_rope_tables(np.arange(s_len)))
    tab_s = jnp.asarray(np.tile(_rope_tables(past + np.arange(t_len)), (1, bs, 1)))
    ck = cache_k.reshape(depth * n_pool, page * N_HEADS, HEAD_W)
    cv = cache_v.reshape(depth * n_pool, page * N_HEADS, V_HEAD)
    c_all = jnp.concatenate([c_prompt, jnp.repeat(c_sample, t_len, axis=0)], axis=0)

    xp, xs = x_prompt, x_sample.reshape(bs * t_len, d)
    kp_l, vp_l, cp_l, ks_l, vs_l, cs_l = [], [], [], [], [], []
    for l in range(depth):
        lam_init = 0.8 - 0.6 * math.exp(-0.3 * l)
        w_in_b, w_out_b = w_in[l].astype(BF16), w_out[l].astype(BF16)
        w_gate_b, w_up_b, w_down_b = w_gate[l].astype(BF16), w_up[l].astype(BF16), w_down[l].astype(BF16)
        lam_params = jnp.stack([lambda_q1[l], lambda_k1[l], lambda_q2[l], lambda_k2[l]])
        g = subln_g[l].reshape(1, V_HEAD)
        ln1 = jnp.stack([ln1_g[l], ln1_b[l]])
        ln2 = jnp.stack([ln2_g[l], ln2_b[l]])

        mod = _adaln(c_all, w_ada[l], b_ada[l])
        mod_p = mod[:bp].reshape(bp, N_MOD, d)
        mod_s = mod[bp:]

        yconv, q, kb, vt, k, v, cst = _prompt_in(xp, mod_p, w_in_b, conv_w[l], tab_p, ts)
        yattn = _prompt_attn(q, kb, vt, lam_params, g, lam_init, tq)
        kp_l.append(k.reshape(bp, s_len, N_HEADS, HEAD_W))
        vp_l.append(v.reshape(bp, s_len, N_HEADS, V_HEAD))
        cp_l.append(cst[-1])

        st = state_conv[l]
        zero = jnp.zeros((bs, 1, cc), F32)
        st1 = jnp.concatenate([st[:, 1:2]] + [zero] * (t_len - 1), axis=1).reshape(bs * t_len, cc)
        st2 = jnp.concatenate([st[:, 0:1], st[:, 1:2]] + [zero] * (t_len - 2), axis=1).reshape(bs * t_len, cc)
        yconv_s, q_s, k_s, v_s, u_s = _sample_in(xs, mod_s, w_in_b, conv_w[l], st1, st2, tab_s, t_len)
        wq = _query_rows(q_s, bs, t_len)
        rows = pl.cdiv(t_len * N_HEADS, LANES) * LANES

        xp, o = _finish_attn(xp, yconv, yattn, mod_p, w_out_b, ln1, w_gate_b, w_up_b, w_down_b, ln2, alpha, tm,
                             page_table, wq, _token_head_rows(k_s, bs, t_len, rows),
                             _token_head_rows(v_s, bs, t_len, rows), ck, cv, l * n_pool, t_len)
        o1, o2 = _component_outputs(o, bs, t_len)
        xs = _sample_finish(xs, yconv_s, o1, o2, lam_params, g, mod_s, w_out_b, ln1, w_gate_b, w_up_b, w_down_b,
                            ln2, alpha, lam_init)
        ks_l.append(k_s.reshape(bs, t_len, N_HEADS, HEAD_W))
        vs_l.append(v_s.reshape(bs, t_len, N_HEADS, V_HEAD))
        cs_l.append(u_s.reshape(bs, t_len, cc)[:, t_len - (CONV_K - 1):])

    return (xp, xs.reshape(bs, t_len, d), jnp.stack(kp_l), jnp.stack(vp_l), jnp.stack(cp_l),
            jnp.stack(ks_l), jnp.stack(vs_l), jnp.stack(cs_l))
```

```python
import functools
import math

import jax
import jax.numpy as jnp
import numpy as np
from jax import lax
from jax.experimental import pallas as pl
from jax.experimental.pallas import tpu as pltpu

F32 = jnp.float32
BF16 = jnp.bfloat16

N_HEADS = 4
QK_HEAD = 64
V_HEAD = 128
HEAD_W = 2 * QK_HEAD
CONV_K = 3
ROT_DIM = QK_HEAD // 4
ROPE_THETA = 500000.0
N_MOD = 6
LN_EPS = 1e-5
LANES = 128
SUBLANES = 8
SUM_ROWS = 16
VMEM_LIMIT_BYTES = 56 * 1024 * 1024
NEG_INF = float("-inf")
Q_SCALE = QK_HEAD ** -0.5 * math.log2(math.e)


def _params(n_axes):
    return pltpu.CompilerParams(dimension_semantics=("arbitrary",) * n_axes,
                                vmem_limit_bytes=VMEM_LIMIT_BYTES)


def _resident(shape):
    return pl.BlockSpec(shape, lambda *_: (0,) * len(shape), pipeline_mode=pl.Buffered(1))


def _largest_divisor(n, cap):
    return max(d for d in range(1, min(n, cap) + 1) if n % d == 0)


def _layer_norm(x, g, b):
    mu = jnp.mean(x, axis=-1, keepdims=True)
    xc = x - mu
    var = jnp.mean(xc * xc, axis=-1, keepdims=True)
    return xc * lax.rsqrt(var + LN_EPS) * g + b


def _adaln_kernel(c_ref, w_ref, b_ref, o_ref):
    a = jax.nn.silu(c_ref[...]).astype(BF16)
    o_ref[...] = jnp.dot(a, w_ref[...].astype(BF16), preferred_element_type=F32) + b_ref[...]


def _adaln(c, w, b):
    n, d = c.shape
    n_out = w.shape[1]
    tn = 1536 if n_out % 1536 == 0 else n_out
    return pl.pallas_call(
        _adaln_kernel,
        grid=(n_out // tn,),
        in_specs=[pl.BlockSpec((n, d), lambda j: (0, 0)),
                  pl.BlockSpec((d, tn), lambda j: (0, j)),
                  pl.BlockSpec((1, tn), lambda j: (0, j))],
        out_specs=pl.BlockSpec((n, tn), lambda j: (0, j)),
        out_shape=jax.ShapeDtypeStruct((n, n_out), F32),
        compiler_params=_params(1),
        name="adaln",
    )(c, w, b.reshape(1, n_out))


def _rope_tables(pos):
    inv = ROPE_THETA ** (-np.arange(0, ROT_DIM, 2, dtype=np.float64) / ROT_DIM)
    ang = np.asarray(pos, np.float64)[:, None] * inv[None, :]
    cos, sin = np.cos(ang), np.sin(ang)
    t, half = ang.shape[0], ROT_DIM // 2
    c = np.concatenate([cos, cos, np.ones((t, QK_HEAD - ROT_DIM))], axis=1)
    s_up = np.concatenate([-sin, np.zeros((t, QK_HEAD - half))], axis=1)
    s_dn = np.concatenate([np.zeros((t, half)), sin, np.zeros((t, QK_HEAD - ROT_DIM))], axis=1)
    return np.stack([np.tile(a, (1, LANES // QK_HEAD)) for a in (c, s_up, s_dn)]).astype(np.float32)


def _rope(x, tab_ref):
    c, s_up, s_dn = tab_ref[0], tab_ref[1], tab_ref[2]
    half = ROT_DIM // 2
    cols = []
    for i in range(x.shape[1] // LANES):
        xc = x[:, i * LANES:(i + 1) * LANES]
        cols.append(xc * c + pltpu.roll(xc, LANES - half, 1) * s_up + pltpu.roll(xc, half, 1) * s_dn)
    return jnp.concatenate(cols, axis=1)


def _prompt_in_kernel(x_ref, mod_ref, w_ref, cw_ref, tab_ref,
                      yconv_ref, q_ref, kb_ref, vt_ref, k_ref, v_ref, cst_ref, carry_ref, ubuf_ref, *, ts, cc):
    s, b = pl.program_id(0), pl.program_id(1)

    @pl.when(s == 0)
    def _():
        carry_ref[b] = jnp.zeros((SUBLANES, cc), F32)

    mod = mod_ref[0]
    xm = (x_ref[0] * (1.0 + mod[1:2]) + mod[0:1]).astype(BF16)

    z = jnp.dot(xm, w_ref[:, 0:3 * cc], preferred_element_type=F32)
    h, gb, gc = z[:, 0:cc], z[:, cc:2 * cc], z[:, 2 * cc:3 * cc]
    u = gc * h

    ubuf_ref[0:SUBLANES] = carry_ref[b]
    ubuf_ref[SUBLANES:SUBLANES + ts] = u
    u1 = ubuf_ref[SUBLANES - 1:SUBLANES - 1 + ts]
    u2 = ubuf_ref[SUBLANES - 2:SUBLANES - 2 + ts]
    cw = cw_ref[...]
    yconv_ref[0] = (gb * (cw[0:1] * u2 + cw[1:2] * u1 + cw[2:3] * u)).astype(BF16)
    carry_ref[b] = u[ts - SUBLANES:]
    cst_ref[0, 0] = u[ts - (CONV_K - 1):]

    z = jnp.dot(xm, w_ref[:, 3 * cc:], preferred_element_type=F32)
    hw = N_HEADS * HEAD_W
    q_ref[0] = (_rope(z[:, 0:hw], tab_ref) * Q_SCALE).astype(BF16)
    k = _rope(z[:, hw:2 * hw], tab_ref)
    v = z[:, 2 * hw:]
    kb_ref[0] = k.astype(BF16)
    vt_ref[0] = v.T.astype(BF16)
    for h in range(N_HEADS):
        rows = pl.ds(h, ts, stride=N_HEADS)
        k_ref[0, rows, :] = k[:, h * HEAD_W:(h + 1) * HEAD_W]
        v_ref[0, rows, :] = v[:, h * V_HEAD:(h + 1) * V_HEAD]


def _prompt_in(x, mod, w_in, conv_w, tab, ts):
    bsz, seq, d = x.shape
    cc = conv_w.shape[1]
    hw = N_HEADS * HEAD_W
    n_in = w_in.shape[1]
    tok = lambda s, b: (b, s, 0)
    kern = functools.partial(_prompt_in_kernel, ts=ts, cc=cc)
    return pl.pallas_call(
        kern,
        grid=(seq // ts, bsz),
        in_specs=[pl.BlockSpec((1, ts, d), tok),
                  pl.BlockSpec((1, N_MOD, d), lambda s, b: (b, 0, 0)),
                  _resident((d, n_in)),
                  _resident((CONV_K, cc)),
                  pl.BlockSpec((3, ts, LANES), lambda s, b: (0, s, 0))],
        out_specs=[pl.BlockSpec((1, ts, cc), tok),
                   pl.BlockSpec((1, ts, hw), tok),
                   pl.BlockSpec((1, ts, hw), tok),
                   pl.BlockSpec((1, N_HEADS * V_HEAD, ts), lambda s, b: (b, 0, s)),
                   pl.BlockSpec((1, ts * N_HEADS, HEAD_W), tok),
                   pl.BlockSpec((1, ts * N_HEADS, V_HEAD), tok),
                   pl.BlockSpec((1, 1, CONV_K - 1, cc), lambda s, b: (s, b, 0, 0))],
        out_shape=[jax.ShapeDtypeStruct((bsz, seq, cc), BF16),
                   jax.ShapeDtypeStruct((bsz, seq, hw), BF16),
                   jax.ShapeDtypeStruct((bsz, seq, hw), BF16),
                   jax.ShapeDtypeStruct((bsz, N_HEADS * V_HEAD, seq), BF16),
                   jax.ShapeDtypeStruct((bsz, seq * N_HEADS, HEAD_W), F32),
                   jax.ShapeDtypeStruct((bsz, seq * N_HEADS, V_HEAD), F32),
                   jax.ShapeDtypeStruct((seq // ts, bsz, CONV_K - 1, cc), F32)],
        scratch_shapes=[pltpu.VMEM((bsz, SUBLANES, cc), F32),
                        pltpu.VMEM((ts + SUBLANES, cc), F32)],
        compiler_params=_params(2),
        name="prompt_in",
    )(x, mod, w_in, conv_w, tab)


def _lambda_value(lam_ref, lam_init):
    lp = lam_ref[...]
    a = jnp.sum(lp[0:1] * lp[1:2], axis=1, keepdims=True)
    b = jnp.sum(lp[2:3] * lp[3:4], axis=1, keepdims=True)
    return jnp.exp(a) - jnp.exp(b) + lam_init


def _head_out(o1, o2, lam, g, lam_init):
    d = o1 - lam * o2
    d = d * lax.rsqrt(jnp.mean(d * d, axis=-1, keepdims=True) + LN_EPS) * g
    return d * (1.0 - lam_init)


def _softmax_step(carry, s, v):
    m, l, acc = carry
    m_new = jnp.maximum(m, jnp.max(s, axis=1, keepdims=True))
    alpha = jnp.exp2(m - m_new)
    p = jnp.exp2(s - m_new)
    l = alpha * l + jnp.sum(p, axis=1, keepdims=True)
    acc = alpha * acc + jnp.dot(p.astype(BF16), v, preferred_element_type=F32)
    return m_new, l, acc


def _prompt_attn_kernel(q_ref, k_ref, vt_ref, lam_ref, g_ref, o_ref, *, seq, tq, lam_init):
    lam = _lambda_value(lam_ref, lam_init)
    lane = lax.broadcasted_iota(jnp.int32, (tq, HEAD_W), 1)
    nt = (((1,), (1,)), ((), ()))
    g_col = g_ref[...]

    def stacked_queries(qi):
        q = q_ref[0, qi * tq:(qi + 1) * tq, :]
        zero = jnp.zeros_like(q)
        return jnp.concatenate([jnp.where(lane < QK_HEAD, q, zero), jnp.where(lane >= QK_HEAD, q, zero)], axis=0)

    ones_rows = jnp.ones((SUM_ROWS, tq), BF16)

    def step(carry, qq, qi, ki):
        m, acc = carry
        keys = slice(ki * tq, (ki + 1) * tq)
        s = lax.dot_general(k_ref[0, keys, :], qq, nt, preferred_element_type=F32)
        if ki == qi:
            key = lax.broadcasted_iota(jnp.int32, s.shape, 0)
            col = lax.broadcasted_iota(jnp.int32, s.shape, 1)
            s = jnp.where(key <= jnp.where(col >= tq, col - tq, col), s, NEG_INF)
        m_new = jnp.maximum(m, jnp.max(s, axis=0, keepdims=True))
        p = jnp.exp2(s - m_new).astype(BF16)
        vt_ones = jnp.concatenate([vt_ref[0, :, keys], ones_rows], axis=0)
        acc = jnp.exp2(m - m_new) * acc + jnp.dot(vt_ones, p, preferred_element_type=F32)
        return m_new, acc

    for qi in range(seq // tq):
        qq = stacked_queries(qi)
        carry = (jnp.full((1, 2 * tq), NEG_INF, F32), jnp.zeros((V_HEAD + SUM_ROWS, 2 * tq), F32))
        for ki in range(qi + 1):
            carry = step(carry, qq, qi, ki)
        _, acc = carry
        o = acc[:V_HEAD] / acc[V_HEAD:V_HEAD + 1]
        d = o[:, :tq] - lam * o[:, tq:]
        d = d * lax.rsqrt(jnp.mean(d * d, axis=0, keepdims=True) + LN_EPS) * g_col * (1.0 - lam_init)
        o_ref[0, qi * tq:(qi + 1) * tq, :] = d.T.astype(BF16)


def _prompt_attn(q, k, vt, lam_params, g, lam_init, tq):
    bsz, seq, _ = q.shape
    blk = pl.BlockSpec((1, seq, HEAD_W), lambda b, h: (b, 0, h))
    kern = functools.partial(_prompt_attn_kernel, seq=seq, tq=tq, lam_init=lam_init)
    return pl.pallas_call(
        kern,
        grid=(bsz, N_HEADS),
        in_specs=[blk, blk,
                  pl.BlockSpec((1, V_HEAD, seq), lambda b, h: (b, h, 0)),
                  pl.BlockSpec((4, QK_HEAD), lambda b, h: (0, 0)),
                  pl.BlockSpec((V_HEAD, 1), lambda b, h: (0, 0))],
        out_specs=blk,
        out_shape=jax.ShapeDtypeStruct((bsz, seq, N_HEADS * V_HEAD), BF16),
        compiler_params=_params(2),
        name="prompt_attn",
    )(q, k, vt, lam_params, g.reshape(V_HEAD, 1))


FINISH_PHASES = 4
SAMPLE_CHUNK_PAGES = 16
SAMPLE_RING_SLOTS = 2


def _finish_math(x, yc, ya, gate1, shift2, scale2, gate2, wo_ref, ln1g, ln1b, wg_ref, wu_ref, wd_ref,
                 ln2g, ln2b, alpha, around_phase=None):
    run = around_phase or (lambda i, phase: phase())
    cc = yc.shape[1]
    v = {}

    def project():
        y = (jnp.dot(yc, wo_ref[0:cc], preferred_element_type=F32)
             + jnp.dot(ya, wo_ref[cc:], preferred_element_type=F32))
        v["x1"] = _layer_norm(alpha * x + gate1 * y, ln1g, ln1b)
        v["hf"] = (v["x1"] * (1.0 + scale2) + shift2).astype(BF16)

    def gate():
        v["g"] = jnp.dot(v["hf"], wg_ref[...], preferred_element_type=F32)

    def up():
        u = jnp.dot(v["hf"], wu_ref[...], preferred_element_type=F32)
        v["act"] = (jax.nn.silu(v["g"]) * u).astype(BF16)

    def down():
        f = jnp.dot(v["act"], wd_ref[...], preferred_element_type=F32)
        v["out"] = _layer_norm(alpha * v["x1"] + gate2 * f, ln2g, ln2b)

    for i, phase in enumerate((project, gate, up, down)):
        run(i, phase)
    return v["out"]


def _finish_attn_kernel(pt_ref, x_ref, yc_ref, ya_ref, mod_ref, wo_ref, ln1_ref, wg_ref, wu_ref, wd_ref, ln2_ref,
                        wq_ref, kn_ref, vn_ref, ck_ref, cv_ref, o_ref, so_ref,
                        kbuf, vbuf, sem, m_ref, l_ref, acc_ref,
                        *, alpha, n_pages, pages_per_chunk, chunks_per_step, page_base, t_len):
    t = pl.program_id(0) * pl.num_programs(1) + pl.program_id(1)
    n_chunks = n_pages // pages_per_chunk
    total_chunks = pl.num_programs(0) * pl.num_programs(1) * chunks_per_step
    n_slots = kbuf.shape[0]
    page_rows = ck_ref.shape[1]
    nt = (((1,), (1,)), ((), ()))
    rows = wq_ref.shape[1]

    def copies(index):
        batch, chunk, slot = index // n_chunks, index % n_chunks, index % n_slots
        out = []
        for g in range(pages_per_chunk):
            pg = pt_ref[batch, chunk * pages_per_chunk + g] + page_base
            dst = pl.ds(g * page_rows, page_rows)
            out.append((pltpu.make_async_copy(ck_ref.at[pg], kbuf.at[slot, dst], sem.at[0, slot]),
                        pltpu.make_async_copy(cv_ref.at[pg], vbuf.at[slot, dst], sem.at[1, slot])))
        return out

    def start(index):
        for k_copy, v_copy in copies(index):
            k_copy.start(priority=0)
            v_copy.start(priority=1)

    def reset_state():
        m_ref[...] = jnp.full(m_ref.shape, NEG_INF, F32)
        l_ref[...] = jnp.zeros(l_ref.shape, F32)
        acc_ref[...] = jnp.zeros(acc_ref.shape, F32)

    @pl.when(t == 0)
    def _():
        reset_state()
        for index in range(n_slots):
            start(index)

    row_head = lax.broadcasted_iota(jnp.int32, (rows, page_rows), 0) // (2 * t_len)
    col_head = lax.broadcasted_iota(jnp.int32, (rows, page_rows), 1) % N_HEADS
    own_head = row_head == col_head

    def attend(s, v):
        m, l, acc = _softmax_step((m_ref[...], l_ref[...], acc_ref[...]), s, v)
        m_ref[...] = m
        l_ref[...] = l
        acc_ref[...] = acc

    def consume(index):
        slot = index % n_slots
        chunk_copies = copies(index)
        for k_copy, _ in chunk_copies:
            k_copy.wait()
        for _, v_copy in chunk_copies:
            v_copy.wait()
        head_bias = jnp.where(own_head, 0.0, NEG_INF)
        chunk_bias = jnp.concatenate([head_bias] * pages_per_chunk, axis=1)
        s = lax.dot_general(wq_ref[0], kbuf[slot].astype(BF16), nt, preferred_element_type=F32) + chunk_bias
        attend(s, vbuf[slot].astype(BF16))

    def refill(index):
        @pl.when(index + n_slots < total_chunks)
        def _():
            start(index + n_slots)

    def around_phase(i, matmul_phase):
        for k in range(chunks_per_step):
            if (0 if 2 * k < chunks_per_step else FINISH_PHASES // 2) == i:
                consume(t * chunks_per_step + k)
                refill(t * chunks_per_step + k)
        matmul_phase()

    mod = mod_ref[0]
    o_ref[0] = _finish_math(x_ref[0], yc_ref[0], ya_ref[0], mod[2:3], mod[3:4], mod[4:5], mod[5:6],
                            wo_ref, ln1_ref[0:1], ln1_ref[1:2], wg_ref, wu_ref, wd_ref,
                            ln2_ref[0:1], ln2_ref[1:2], alpha, around_phase)

    @pl.when(((t + 1) * chunks_per_step) % n_chunks == 0)
    def _():
        s = lax.dot_general(wq_ref[0], kn_ref[0].astype(BF16), nt, preferred_element_type=F32)
        new_rows = kn_ref.shape[1]
        q_step = lax.broadcasted_iota(jnp.int32, (rows, new_rows), 0) % t_len
        k_step = lax.broadcasted_iota(jnp.int32, (rows, new_rows), 1) // N_HEADS
        s = jnp.where(own_head[:, :new_rows] & (k_step <= q_step), s, NEG_INF)
        attend(s, vn_ref[0].astype(BF16))
        so_ref[0] = acc_ref[...] / l_ref[...]
        reset_state()


def _sample_finish_kernel(x_ref, yc_ref, o1_ref, o2_ref, lam_ref, g_ref, mod_ref, wo_ref, ln1_ref,
                          wg_ref, wu_ref, wd_ref, ln2_ref, o_ref, *, alpha, lam_init):
    lam = _lambda_value(lam_ref, lam_init)
    o1, o2 = o1_ref[...], o2_ref[...]
    ya = jnp.concatenate(
        [_head_out(o1[:, h * V_HEAD:(h + 1) * V_HEAD], o2[:, h * V_HEAD:(h + 1) * V_HEAD], lam, g_ref[...], lam_init)
         for h in range(N_HEADS)], axis=1).astype(BF16)
    d = x_ref.shape[1]
    gate1, shift2, scale2, gate2 = (mod_ref[:, i * d:(i + 1) * d] for i in range(2, N_MOD))
    o_ref[...] = _finish_math(x_ref[...], yc_ref[...], ya, gate1, shift2, scale2, gate2,
                              wo_ref, ln1_ref[0:1], ln1_ref[1:2], wg_ref, wu_ref, wd_ref,
                              ln2_ref[0:1], ln2_ref[1:2], alpha)


def _finish_attn(x, yc, ya, mod, w_out, ln1, w_gate, w_up, w_down, ln2, alpha, tm,
                 page_table, wq, k_new, v_new, cache_k, cache_v, page_base, t_len):
    bsz, seq, d = x.shape
    cc, d_ff = yc.shape[2], w_gate.shape[1]
    n_s = seq // tm
    sample_bsz, n_pages = page_table.shape
    _, page_rows, width = cache_k.shape
    rows = wq.shape[1]
    pages_per_step, ragged = divmod(sample_bsz * n_pages, bsz * n_s)
    assert not ragged and pages_per_step > 0
    g = _largest_divisor(math.gcd(pages_per_step, n_pages), SAMPLE_CHUNK_PAGES)
    chunks_per_step, n_chunks = pages_per_step // g, n_pages // g
    n_slots = SAMPLE_RING_SLOTS
    assert bsz * n_s * chunks_per_step >= n_slots
    kern = functools.partial(_finish_attn_kernel, alpha=alpha, n_pages=n_pages, pages_per_chunk=g,
                             chunks_per_step=chunks_per_step, page_base=page_base, t_len=t_len)
    tok = lambda b, s, pt: (b, s, 0)
    fixed = lambda shape: pl.BlockSpec(shape, lambda b, s, pt: (0,) * len(shape), pipeline_mode=pl.Buffered(1))
    sample = lambda r: pl.BlockSpec(
        (1, r, width), lambda b, s, pt: (((b * n_s + s) * chunks_per_step) // n_chunks, 0, 0))
    grid_spec = pltpu.PrefetchScalarGridSpec(
        num_scalar_prefetch=1,
        grid=(bsz, n_s),
        in_specs=[pl.BlockSpec((1, tm, d), tok),
                  pl.BlockSpec((1, tm, cc), tok),
                  pl.BlockSpec((1, tm, d - cc), tok),
                  pl.BlockSpec((1, N_MOD, d), lambda b, s, pt: (b, 0, 0)),
                  fixed((d, d)), fixed((2, d)),
                  fixed((d, d_ff)), fixed((d, d_ff)), fixed((d_ff, d)), fixed((2, d)),
                  sample(rows), sample(k_new.shape[1]), sample(v_new.shape[1]),
                  pl.BlockSpec(memory_space=pl.ANY),
                  pl.BlockSpec(memory_space=pl.ANY)],
        out_specs=[pl.BlockSpec((1, tm, d), tok), sample(rows)],
        scratch_shapes=[pltpu.VMEM((n_slots, g * page_rows, width), F32),
                        pltpu.VMEM((n_slots, g * page_rows, width), F32),
                        pltpu.SemaphoreType.DMA((2, n_slots)),
                        pltpu.VMEM((rows, 1), F32), pltpu.VMEM((rows, 1), F32), pltpu.VMEM((rows, V_HEAD), F32)],
    )
    return pl.pallas_call(
        kern,
        grid_spec=grid_spec,
        out_shape=[jax.ShapeDtypeStruct((bsz, seq, d), F32),
                   jax.ShapeDtypeStruct((sample_bsz, rows, width), F32)],
        compiler_params=_params(2),
        name="finish_attn",
    )(page_table, x, yc, ya, mod, w_out, ln1, w_gate, w_up, w_down, ln2, wq, k_new, v_new, cache_k, cache_v)


def _sample_finish(x, yc, o1, o2, lam_params, g, mod_rows, w_out, ln1, w_gate, w_up, w_down, ln2, alpha, lam_init):
    n, d = x.shape
    kern = functools.partial(_sample_finish_kernel, alpha=alpha, lam_init=lam_init)
    args = (x, yc, o1, o2, lam_params, g, mod_rows, w_out, ln1, w_gate, w_up, w_down, ln2)
    return pl.pallas_call(
        kern,
        grid=(1,),
        in_specs=[_resident(a.shape) for a in args],
        out_specs=pl.BlockSpec((n, d), lambda i: (0, 0)),
        out_shape=jax.ShapeDtypeStruct((n, d), F32),
        compiler_params=_params(1),
        name="sample_finish",
    )(*args)


def _sample_in_kernel(x_ref, mod_ref, w_ref, cw_ref, st1_ref, st2_ref, tab_ref,
                      yconv_ref, q_ref, k_ref, v_ref, u_ref, *, t_len, cc):
    d = x_ref.shape[1]
    xm = (x_ref[...] * (1.0 + mod_ref[:, d:2 * d]) + mod_ref[:, 0:d]).astype(BF16)
    z = jnp.dot(xm, w_ref[...], preferred_element_type=F32)
    h, gb, gc = z[:, 0:cc], z[:, cc:2 * cc], z[:, 2 * cc:3 * cc]
    u = gc * h
    n = u.shape[0]
    t = lax.broadcasted_iota(jnp.int32, (n, cc), 0) % t_len
    u1 = jnp.where(t >= 1, pltpu.roll(u, 1, 0), st1_ref[...])
    u2 = jnp.where(t >= 2, pltpu.roll(u, 2, 0), st2_ref[...])
    cw = cw_ref[...]
    yconv_ref[...] = (gb * (cw[0:1] * u2 + cw[1:2] * u1 + cw[2:3] * u)).astype(BF16)
    u_ref[...] = u
    hw = N_HEADS * HEAD_W
    q_ref[...] = (_rope(z[:, 3 * cc:3 * cc + hw], tab_ref) * Q_SCALE).astype(BF16)
    k_ref[...] = _rope(z[:, 3 * cc + hw:3 * cc + 2 * hw], tab_ref)
    v_ref[...] = z[:, 3 * cc + 2 * hw:]


def _sample_in(x, mod_rows, w_in, conv_w, st1, st2, tab, t_len):
    n, _ = x.shape
    cc = conv_w.shape[1]
    hw = N_HEADS * HEAD_W
    kern = functools.partial(_sample_in_kernel, t_len=t_len, cc=cc)
    args = (x, mod_rows, w_in, conv_w, st1, st2, tab)
    out_shapes = [jax.ShapeDtypeStruct((n, cc), BF16), jax.ShapeDtypeStruct((n, hw), BF16),
                  jax.ShapeDtypeStruct((n, hw), F32), jax.ShapeDtypeStruct((n, N_HEADS * V_HEAD), F32),
                  jax.ShapeDtypeStruct((n, cc), F32)]
    return pl.pallas_call(
        kern,
        grid=(1,),
        in_specs=[_resident(a.shape) for a in args],
        out_specs=[pl.BlockSpec(o.shape, lambda i: (0, 0)) for o in out_shapes],
        out_shape=out_shapes,
        compiler_params=_params(1),
        name="sample_in",
    )(*args)


def _query_rows(q, bsz, t_len):
    q5 = q.reshape(bsz, t_len, N_HEADS, 2, QK_HEAD).transpose(0, 2, 3, 1, 4)
    eye_c = jnp.eye(2, dtype=q.dtype)
    w = q5[:, :, :, :, None, :] * eye_c[None, None, :, None, :, None]
    return w.reshape(bsz, N_HEADS * 2 * t_len, HEAD_W)


def _token_head_rows(a, bsz, t_len, rows):
    a = a.reshape(bsz, t_len * N_HEADS, a.shape[1] // N_HEADS)
    return jnp.pad(a, ((0, 0), (0, rows - t_len * N_HEADS), (0, 0)))


def _component_outputs(o, bsz, t_len):
    o5 = o.reshape(bsz, N_HEADS, 2, t_len, V_HEAD).transpose(2, 0, 3, 1, 4)
    o5 = o5.reshape(2, bsz * t_len, N_HEADS * V_HEAD)
    return o5[0], o5[1]


def kernel(x_prompt, x_sample, c_prompt, c_sample, cache_k, cache_v, state_conv, page_table, w_ada, b_ada, w_in, conv_w, lambda_q1, lambda_k1, lambda_q2, lambda_k2, subln_g, w_out, ln1_g, ln1_b, w_gate, w_up, w_down, ln2_g, ln2_b):
    bp, s_len, d = x_prompt.shape
    bs, t_len, _ = x_sample.shape
    depth, n_pool, page = cache_k.shape[:3]
    past = page_table.shape[1] * page
    cc = conv_w.shape[2]
    alpha = (2.0 * depth) ** 0.25
    assert (bs * t_len) % SUBLANES == 0 and t_len <= page

    ts = min(1024, s_len)
    tq = min(1024, s_len)
    tm = min(512, s_len)

    tab_p = jnp.asarray(_rope_tables(np.arange(s_len)))
    tab_s = jnp.asarray(np.tile(_rope_tables(past + np.arange(t_len)), (1, bs, 1)))
    ck = cache_k.reshape(depth * n_pool, page * N_HEADS, HEAD_W)
    cv = cache_v.reshape(depth * n_pool, page * N_HEADS, V_HEAD)
    c_all = jnp.concatenate([c_prompt, jnp.repeat(c_sample, t_len, axis=0)], axis=0)

    xp, xs = x_prompt, x_sample.reshape(bs * t_len, d)
    kp_l, vp_l, cp_l, ks_l, vs_l, cs_l = [], [], [], [], [], []
    for l in range(depth):
        lam_init = 0.8 - 0.6 * math.exp(-0.3 * l)
        w_in_b, w_out_b = w_in[l].astype(BF16), w_out[l].astype(BF16)
        w_gate_b, w_up_b, w_down_b = w_gate[l].astype(BF16), w_up[l].astype(BF16), w_down[l].astype(BF16)
        lam_params = jnp.stack([lambda_q1[l], lambda_k1[l], lambda_q2[l], lambda_k2[l]])
        g = subln_g[l].reshape(1, V_HEAD)
        ln1 = jnp.stack([ln1_g[l], ln1_b[l]])
        ln2 = jnp.stack([ln2_g[l], ln2_b[l]])

        mod = _adaln(c_all, w_ada[l], b_ada[l])
        mod_p = mod[:bp].reshape(bp, N_MOD, d)
        mod_s = mod[bp:]

        yconv, q, kb, vt, k, v, cst = _prompt_in(xp, mod_p, w_in_b, conv_w[l], tab_p, ts)
        yattn = _prompt_attn(q, kb, vt, lam_params, g, lam_init, tq)
        kp_l.append(k.reshape(bp, s_len, N_HEADS, HEAD_W))
        vp_l.append(v.reshape(bp, s_len, N_HEADS, V_HEAD))
        cp_l.append(cst[-1])

        st = state_conv[l]
        zero = jnp.zeros((bs, 1, cc), F32)
        st1 = jnp.concatenate([st[:, 1:2]] + [zero] * (t_len - 1), axis=1).reshape(bs * t_len, cc)
        st2 = jnp.concatenate([st[:, 0:1], st[:, 1:2]] + [zero] * (t_len - 2), axis=1).reshape(bs * t_len, cc)
        yconv_s, q_s, k_s, v_s, u_s = _sample_in(xs, mod_s, w_in_b, conv_w[l], st1, st2, tab_s, t_len)
        wq = _query_rows(q_s, bs, t_len)
        rows = pl.cdiv(t_len * N_HEADS, LANES) * LANES

        xp, o = _finish_attn(xp, yconv, yattn, mod_p, w_out_b, ln1, w_gate_b, w_up_b, w_down_b, ln2, alpha, tm,
                             page_table, wq, _token_head_rows(k_s, bs, t_len, rows),
                             _token_head_rows(v_s, bs, t_len, rows), ck, cv, l * n_pool, t_len)
        o1, o2 = _component_outputs(o, bs, t_len)
        xs = _sample_finish(xs, yconv_s, o1, o2, lam_params, g, mod_s, w_out_b, ln1, w_gate_b, w_up_b, w_down_b,
                            ln2, alpha, lam_init)
        ks_l.append(k_s.reshape(bs, t_len, N_HEADS, HEAD_W))
        vs_l.append(v_s.reshape(bs, t_len, N_HEADS, V_HEAD))
        cs_l.append(u_s.reshape(bs, t_len, cc)[:, t_len - (CONV_K - 1):])

    return (xp, xs.reshape(bs, t_len, d), jnp.stack(kp_l), jnp.stack(vp_l), jnp.stack(cp_l),
            jnp.stack(ks_l), jnp.stack(vs_l), jnp.stack(cs_l))
```

```python
import functools
import math

import jax
import jax.numpy as jnp
import numpy as np
from jax import lax
from jax.experimental import pallas as pl
from jax.experimental.pallas import tpu as pltpu

F32 = jnp.float32
BF16 = jnp.bfloat16

N_HEADS = 4
QK_HEAD = 64
V_HEAD = 128
HEAD_W = 2 * QK_HEAD
CONV_K = 3
ROT_DIM = QK_HEAD // 4
ROPE_THETA = 500000.0
N_MOD = 6
LN_EPS = 1e-5
LANES = 128
SUBLANES = 8
SUM_ROWS = 16
ATTN_HEADS_PER_STEP = 2
VMEM_LIMIT_BYTES = 56 * 1024 * 1024
NEG_INF = float("-inf")
Q_SCALE = QK_HEAD ** -0.5 * math.log2(math.e)


def _params(n_axes):
    return pltpu.CompilerParams(dimension_semantics=("arbitrary",) * n_axes,
                                vmem_limit_bytes=VMEM_LIMIT_BYTES)


def _resident(shape):
    return pl.BlockSpec(shape, lambda *_: (0,) * len(shape), pipeline_mode=pl.Buffered(1))


def _largest_divisor(n, cap):
    return max(d for d in range(1, min(n, cap) + 1) if n % d == 0)


def _layer_norm(x, g, b):
    mu = jnp.mean(x, axis=-1, keepdims=True)
    xc = x - mu
    var = jnp.mean(xc * xc, axis=-1, keepdims=True)
    return xc * lax.rsqrt(var + LN_EPS) * g + b


def _adaln_kernel(c_ref, w_ref, b_ref, o_ref):
    a = jax.nn.silu(c_ref[...]).astype(BF16)
    o_ref[...] = jnp.dot(a, w_ref[...].astype(BF16), preferred_element_type=F32) + b_ref[...]


def _adaln(c, w, b):
    n, d = c.shape
    n_out = w.shape[1]
    tn = 1536 if n_out % 1536 == 0 else n_out
    return pl.pallas_call(
        _adaln_kernel,
        grid=(n_out // tn,),
        in_specs=[pl.BlockSpec((n, d), lambda j: (0, 0)),
                  pl.BlockSpec((d, tn), lambda j: (0, j)),
                  pl.BlockSpec((1, tn), lambda j: (0, j))],
        out_specs=pl.BlockSpec((n, tn), lambda j: (0, j)),
        out_shape=jax.ShapeDtypeStruct((n, n_out), F32),
        compiler_params=_params(1),
        name="adaln",
    )(c, w, b.reshape(1, n_out))


def _rope_tables(pos):
    inv = ROPE_THETA ** (-np.arange(0, ROT_DIM, 2, dtype=np.float64) / ROT_DIM)
    ang = np.asarray(pos, np.float64)[:, None] * inv[None, :]
    cos, sin = np.cos(ang), np.sin(ang)
    t, half = ang.shape[0], ROT_DIM // 2
    c = np.concatenate([cos, cos, np.ones((t, QK_HEAD - ROT_DIM))], axis=1)
    s_up = np.concatenate([-sin, np.zeros((t, QK_HEAD - half))], axis=1)
    s_dn = np.concatenate([np.zeros((t, half)), sin, np.zeros((t, QK_HEAD - ROT_DIM))], axis=1)
    return np.stack([np.tile(a, (1, LANES // QK_HEAD)) for a in (c, s_up, s_dn)]).astype(np.float32)


def _rope(x, tab_ref):
    c, s_up, s_dn = tab_ref[0], tab_ref[1], tab_ref[2]
    half = ROT_DIM // 2
    cols = []
    for i in range(x.shape[1] // LANES):
        xc = x[:, i * LANES:(i + 1) * LANES]
        cols.append(xc * c + pltpu.roll(xc, LANES - half, 1) * s_up + pltpu.roll(xc, half, 1) * s_dn)
    return jnp.concatenate(cols, axis=1)


def _prompt_in_kernel(x_ref, mod_ref, w_ref, cw_ref, tab_ref,
                      yconv_ref, q_ref, kb_ref, vt_ref, k_ref, v_ref, cst_ref, carry_ref, ubuf_ref, *, ts, cc):
    s, b = pl.program_id(0), pl.program_id(1)

    @pl.when(s == 0)
    def _():
        carry_ref[b] = jnp.zeros((SUBLANES, cc), F32)

    mod = mod_ref[0]
    xm = (x_ref[0] * (1.0 + mod[1:2]) + mod[0:1]).astype(BF16)

    z = jnp.dot(xm, w_ref[:, 0:3 * cc], preferred_element_type=F32)
    h, gb, gc = z[:, 0:cc], z[:, cc:2 * cc], z[:, 2 * cc:3 * cc]
    u = gc * h

    ubuf_ref[0:SUBLANES] = carry_ref[b]
    ubuf_ref[SUBLANES:SUBLANES + ts] = u
    u1 = ubuf_ref[SUBLANES - 1:SUBLANES - 1 + ts]
    u2 = ubuf_ref[SUBLANES - 2:SUBLANES - 2 + ts]
    cw = cw_ref[...]
    yconv_ref[0] = (gb * (cw[0:1] * u2 + cw[1:2] * u1 + cw[2:3] * u)).astype(BF16)
    carry_ref[b] = u[ts - SUBLANES:]
    cst_ref[0, 0] = u[ts - (CONV_K - 1):]

    z = jnp.dot(xm, w_ref[:, 3 * cc:], preferred_element_type=F32)
    hw = N_HEADS * HEAD_W
    q_ref[0] = (_rope(z[:, 0:hw], tab_ref) * Q_SCALE).astype(BF16)
    k = _rope(z[:, hw:2 * hw], tab_ref)
    v = z[:, 2 * hw:]
    kb_ref[0] = k.astype(BF16)
    vt_ref[0] = v.T.astype(BF16)
    for h in range(N_HEADS):
        rows = pl.ds(h, ts, stride=N_HEADS)
        k_ref[0, rows, :] = k[:, h * HEAD_W:(h + 1) * HEAD_W]
        v_ref[0, rows, :] = v[:, h * V_HEAD:(h + 1) * V_HEAD]


def _prompt_in(x, mod, w_in, conv_w, tab, ts):
    bsz, seq, d = x.shape
    cc = conv_w.shape[1]
    hw = N_HEADS * HEAD_W
    n_in = w_in.shape[1]
    tok = lambda s, b: (b, s, 0)
    kern = functools.partial(_prompt_in_kernel, ts=ts, cc=cc)
    return pl.pallas_call(
        kern,
        grid=(seq // ts, bsz),
        in_specs=[pl.BlockSpec((1, ts, d), tok),
                  pl.BlockSpec((1, N_MOD, d), lambda s, b: (b, 0, 0)),
                  _resident((d, n_in)),
                  _resident((CONV_K, cc)),
                  pl.BlockSpec((3, ts, LANES), lambda s, b: (0, s, 0))],
        out_specs=[pl.BlockSpec((1, ts, cc), tok),
                   pl.BlockSpec((1, ts, hw), tok),
                   pl.BlockSpec((1, ts, hw), tok),
                   pl.BlockSpec((1, N_HEADS * V_HEAD, ts), lambda s, b: (b, 0, s)),
                   pl.BlockSpec((1, ts * N_HEADS, HEAD_W), tok),
                   pl.BlockSpec((1, ts * N_HEADS, V_HEAD), tok),
                   pl.BlockSpec((1, 1, CONV_K - 1, cc), lambda s, b: (s, b, 0, 0))],
        out_shape=[jax.ShapeDtypeStruct((bsz, seq, cc), BF16),
                   jax.ShapeDtypeStruct((bsz, seq, hw), BF16),
                   jax.ShapeDtypeStruct((bsz, seq, hw), BF16),
                   jax.ShapeDtypeStruct((bsz, N_HEADS * V_HEAD, seq), BF16),
                   jax.ShapeDtypeStruct((bsz, seq * N_HEADS, HEAD_W), F32),
                   jax.ShapeDtypeStruct((bsz, seq * N_HEADS, V_HEAD), F32),
                   jax.ShapeDtypeStruct((seq // ts, bsz, CONV_K - 1, cc), F32)],
        scratch_shapes=[pltpu.VMEM((bsz, SUBLANES, cc), F32),
                        pltpu.VMEM((ts + SUBLANES, cc), F32)],
        compiler_params=_params(2),
        name="prompt_in",
    )(x, mod, w_in, conv_w, tab)


def _lambda_value(lam_ref, lam_init):
    lp = lam_ref[...]
    a = jnp.sum(lp[0:1] * lp[1:2], axis=1, keepdims=True)
    b = jnp.sum(lp[2:3] * lp[3:4], axis=1, keepdims=True)
    return jnp.exp(a) - jnp.exp(b) + lam_init


def _head_out(o1, o2, lam, g, lam_init):
    d = o1 - lam * o2
    d = d * lax.rsqrt(jnp.mean(d * d, axis=-1, keepdims=True) + LN_EPS) * g
    return d * (1.0 - lam_init)


def _softmax_step(carry, s, v):
    m, l, acc = carry
    m_new = jnp.maximum(m, jnp.max(s, axis=1, keepdims=True))
    alpha = jnp.exp2(m - m_new)
    p = jnp.exp2(s - m_new)
    l = alpha * l + jnp.sum(p, axis=1, keepdims=True)
    acc = alpha * acc + jnp.dot(p.astype(BF16), v, preferred_element_type=F32)
    return m_new, l, acc


def _prompt_attn_kernel(q_ref, k_ref, vt_ref, lam_ref, g_ref, o_ref, *, seq, tq, lam_init):
    lam = _lambda_value(lam_ref, lam_init)
    lane = lax.broadcasted_iota(jnp.int32, (tq, HEAD_W), 1)
    nt = (((1,), (1,)), ((), ()))
    g_col = g_ref[...]

    def stacked_queries(qi, head):
        q = q_ref[0, qi * tq:(qi + 1) * tq, head]
        zero = jnp.zeros_like(q)
        return jnp.concatenate([jnp.where(lane < QK_HEAD, q, zero), jnp.where(lane >= QK_HEAD, q, zero)], axis=0)

    ones_rows = jnp.ones((SUM_ROWS, tq), BF16)

    def step(carry, qq, qi, ki, head):
        m, acc = carry
        keys = slice(ki * tq, (ki + 1) * tq)
        s = lax.dot_general(k_ref[0, keys, head], qq, nt, preferred_element_type=F32)
        if ki == qi:
            key = lax.broadcasted_iota(jnp.int32, s.shape, 0)
            col = lax.broadcasted_iota(jnp.int32, s.shape, 1)
            s = jnp.where(key <= jnp.where(col >= tq, col - tq, col), s, NEG_INF)
        m_new = jnp.maximum(m, jnp.max(s, axis=0, keepdims=True))
        p = jnp.exp2(s - m_new).astype(BF16)
        vt_ones = jnp.concatenate([vt_ref[0, head, keys], ones_rows], axis=0)
        acc = jnp.exp2(m - m_new) * acc + jnp.dot(vt_ones, p, preferred_element_type=F32)
        return m_new, acc

    heads = [slice(i * HEAD_W, (i + 1) * HEAD_W) for i in range(q_ref.shape[2] // HEAD_W)]
    for qi in range(seq // tq):
        qq = [stacked_queries(qi, head) for head in heads]
        carry = [(jnp.full((1, 2 * tq), NEG_INF, F32), jnp.zeros((V_HEAD + SUM_ROWS, 2 * tq), F32)) for _ in heads]
        for ki in range(qi + 1):
            for i, head in enumerate(heads):
                carry[i] = step(carry[i], qq[i], qi, ki, head)
        for (_, acc), head in zip(carry, heads):
            o = acc[:V_HEAD] / acc[V_HEAD:V_HEAD + 1]
            d = o[:, :tq] - lam * o[:, tq:]
            d = d * lax.rsqrt(jnp.mean(d * d, axis=0, keepdims=True) + LN_EPS) * g_col * (1.0 - lam_init)
            o_ref[0, qi * tq:(qi + 1) * tq, head] = d.T.astype(BF16)


def _prompt_attn(q, k, vt, lam_params, g, lam_init, tq):
    bsz, seq, _ = q.shape
    hps = ATTN_HEADS_PER_STEP
    blk = pl.BlockSpec((1, seq, hps * HEAD_W), lambda b, h: (b, 0, h))
    kern = functools.partial(_prompt_attn_kernel, seq=seq, tq=tq, lam_init=lam_init)
    return pl.pallas_call(
        kern,
        grid=(bsz, N_HEADS // hps),
        in_specs=[blk, blk,
                  pl.BlockSpec((1, hps * V_HEAD, seq), lambda b, h: (b, h, 0)),
                  pl.BlockSpec((4, QK_HEAD), lambda b, h: (0, 0)),
                  pl.BlockSpec((V_HEAD, 1), lambda b, h: (0, 0))],
        out_specs=blk,
        out_shape=jax.ShapeDtypeStruct((bsz, seq, N_HEADS * V_HEAD), BF16),
        compiler_params=_params(2),
        name="prompt_attn",
    )(q, k, vt, lam_params, g.reshape(V_HEAD, 1))


FINISH_PHASES = 4
SAMPLE_CHUNK_PAGES = 16
SAMPLE_RING_SLOTS = 2


def _finish_math(x, yc, ya, gate1, shift2, scale2, gate2, wo_ref, ln1g, ln1b, wg_ref, wu_ref, wd_ref,
                 ln2g, ln2b, alpha, around_phase=None):
    run = around_phase or (lambda i, phase: phase())
    cc = yc.shape[1]
    v = {}

    def project():
        y = (jnp.dot(yc, wo_ref[0:cc], preferred_element_type=F32)
             + jnp.dot(ya, wo_ref[cc:], preferred_element_type=F32))
        v["x1"] = _layer_norm(alpha * x + gate1 * y, ln1g, ln1b)
        v["hf"] = (v["x1"] * (1.0 + scale2) + shift2).astype(BF16)

    def gate():
        v["g"] = jnp.dot(v["hf"], wg_ref[...], preferred_element_type=F32)

    def up():
        u = jnp.dot(v["hf"], wu_ref[...], preferred_element_type=F32)
        v["act"] = (jax.nn.silu(v["g"]) * u).astype(BF16)

    def down():
        f = jnp.dot(v["act"], wd_ref[...], preferred_element_type=F32)
        v["out"] = _layer_norm(alpha * v["x1"] + gate2 * f, ln2g, ln2b)

    for i, phase in enumerate((project, gate, up, down)):
        run(i, phase)
    return v["out"]


def _finish_attn_kernel(pt_ref, x_ref, yc_ref, ya_ref, mod_ref, wo_ref, ln1_ref, wg_ref, wu_ref, wd_ref, ln2_ref,
                        wq_ref, kn_ref, vn_ref, ck_ref, cv_ref, o_ref, so_ref,
                        kbuf, vbuf, sem, m_ref, l_ref, acc_ref,
                        *, alpha, n_pages, pages_per_chunk, chunks_per_step, page_base, t_len):
    t = pl.program_id(0) * pl.num_programs(1) + pl.program_id(1)
    n_chunks = n_pages // pages_per_chunk
    total_chunks = pl.num_programs(0) * pl.num_programs(1) * chunks_per_step
    n_slots = kbuf.shape[0]
    page_rows = ck_ref.shape[1]
    nt = (((1,), (1,)), ((), ()))
    rows = wq_ref.shape[1]

    def copies(index):
        batch, chunk, slot = index // n_chunks, index % n_chunks, index % n_slots
        out = []
        for g in range(pages_per_chunk):
            pg = pt_ref[batch, chunk * pages_per_chunk + g] + page_base
            dst = pl.ds(g * page_rows, page_rows)
            out.append((pltpu.make_async_copy(ck_ref.at[pg], kbuf.at[slot, dst], sem.at[0, slot]),
                        pltpu.make_async_copy(cv_ref.at[pg], vbuf.at[slot, dst], sem.at[1, slot])))
        return out

    def start(index):
        for k_copy, v_copy in copies(index):
            k_copy.start(priority=0)
            v_copy.start(priority=1)

    def reset_state():
        m_ref[...] = jnp.full(m_ref.shape, NEG_INF, F32)
        l_ref[...] = jnp.zeros(l_ref.shape, F32)
        acc_ref[...] = jnp.zeros(acc_ref.shape, F32)

    @pl.when(t == 0)
    def _():
        reset_state()
        for index in range(n_slots):
            start(index)

    row_head = lax.broadcasted_iota(jnp.int32, (rows, page_rows), 0) // (2 * t_len)
    col_head = lax.broadcasted_iota(jnp.int32, (rows, page_rows), 1) % N_HEADS
    own_head = row_head == col_head

    def attend(s, v):
        m, l, acc = _softmax_step((m_ref[...], l_ref[...], acc_ref[...]), s, v)
        m_ref[...] = m
        l_ref[...] = l
        acc_ref[...] = acc

    def consume(index):
        slot = index % n_slots
        chunk_copies = copies(index)
        for k_copy, _ in chunk_copies:
            k_copy.wait()
        for _, v_copy in chunk_copies:
            v_copy.wait()
        head_bias = jnp.where(own_head, 0.0, NEG_INF)
        chunk_bias = jnp.concatenate([head_bias] * pages_per_chunk, axis=1)
        s = lax.dot_general(wq_ref[0], kbuf[slot].astype(BF16), nt, preferred_element_type=F32) + chunk_bias
        attend(s, vbuf[slot].astype(BF16))

    def refill(index):
        @pl.when(index + n_slots < total_chunks)
        def _():
            start(index + n_slots)

    def around_phase(i, matmul_phase):
        for k in range(chunks_per_step):
            if (0 if 2 * k < chunks_per_step else FINISH_PHASES // 2) == i:
                consume(t * chunks_per_step + k)
                refill(t * chunks_per_step + k)
        matmul_phase()

    mod = mod_ref[0]
    o_ref[0] = _finish_math(x_ref[0], yc_ref[0], ya_ref[0], mod[2:3], mod[3:4], mod[4:5], mod[5:6],
                            wo_ref, ln1_ref[0:1], ln1_ref[1:2], wg_ref, wu_ref, wd_ref,
                            ln2_ref[0:1], ln2_ref[1:2], alpha, around_phase)

    @pl.when(((t + 1) * chunks_per_step) % n_chunks == 0)
    def _():
        s = lax.dot_general(wq_ref[0], kn_ref[0].astype(BF16), nt, preferred_element_type=F32)
        new_rows = kn_ref.shape[1]
        q_step = lax.broadcasted_iota(jnp.int32, (rows, new_rows), 0) % t_len
        k_step = lax.broadcasted_iota(jnp.int32, (rows, new_rows), 1) // N_HEADS
        s = jnp.where(own_head[:, :new_rows] & (k_step <= q_step), s, NEG_INF)
        attend(s, vn_ref[0].astype(BF16))
        so_ref[0] = acc_ref[...] / l_ref[...]
        reset_state()


def _sample_finish_kernel(x_ref, yc_ref, o1_ref, o2_ref, lam_ref, g_ref, mod_ref, wo_ref, ln1_ref,
                          wg_ref, wu_ref, wd_ref, ln2_ref, o_ref, *, alpha, lam_init):
    lam = _lambda_value(lam_ref, lam_init)
    o1, o2 = o1_ref[...], o2_ref[...]
    ya = jnp.concatenate(
        [_head_out(o1[:, h * V_HEAD:(h + 1) * V_HEAD], o2[:, h * V_HEAD:(h + 1) * V_HEAD], lam, g_ref[...], lam_init)
         for h in range(N_HEADS)], axis=1).astype(BF16)
    d = x_ref.shape[1]
    gate1, shift2, scale2, gate2 = (mod_ref[:, i * d:(i + 1) * d] for i in range(2, N_MOD))
    o_ref[...] = _finish_math(x_ref[...], yc_ref[...], ya, gate1, shift2, scale2, gate2,
                              wo_ref, ln1_ref[0:1], ln1_ref[1:2], wg_ref, wu_ref, wd_ref,
                              ln2_ref[0:1], ln2_ref[1:2], alpha)


def _finish_attn(x, yc, ya, mod, w_out, ln1, w_gate, w_up, w_down, ln2, alpha, tm,
                 page_table, wq, k_new, v_new, cache_k, cache_v, page_base, t_len):
    bsz, seq, d = x.shape
    cc, d_ff = yc.shape[2], w_gate.shape[1]
    n_s = seq // tm
    sample_bsz, n_pages = page_table.shape
    _, page_rows, width = cache_k.shape
    rows = wq.shape[1]
    pages_per_step, ragged = divmod(sample_bsz * n_pages, bsz * n_s)
    assert not ragged and pages_per_step > 0
    g = _largest_divisor(math.gcd(pages_per_step, n_pages), SAMPLE_CHUNK_PAGES)
    chunks_per_step, n_chunks = pages_per_step // g, n_pages // g
    n_slots = SAMPLE_RING_SLOTS
    assert bsz * n_s * chunks_per_step >= n_slots
    kern = functools.partial(_finish_attn_kernel, alpha=alpha, n_pages=n_pages, pages_per_chunk=g,
                             chunks_per_step=chunks_per_step, page_base=page_base, t_len=t_len)
    tok = lambda b, s, pt: (b, s, 0)
    fixed = lambda shape: pl.BlockSpec(shape, lambda b, s, pt: (0,) * len(shape), pipeline_mode=pl.Buffered(1))
    sample = lambda r: pl.BlockSpec(
        (1, r, width), lambda b, s, pt: (((b * n_s + s) * chunks_per_step) // n_chunks, 0, 0))
    grid_spec = pltpu.PrefetchScalarGridSpec(
        num_scalar_prefetch=1,
        grid=(bsz, n_s),
        in_specs=[pl.BlockSpec((1, tm, d), tok),
                  pl.BlockSpec((1, tm, cc), tok),
                  pl.BlockSpec((1, tm, d - cc), tok),
                  pl.BlockSpec((1, N_MOD, d), lambda b, s, pt: (b, 0, 0)),
                  fixed((d, d)), fixed((2, d)),
                  fixed((d, d_ff)), fixed((d, d_ff)), fixed((d_ff, d)), fixed((2, d)),
                  sample(rows), sample(k_new.shape[1]), sample(v_new.shape[1]),
                  pl.BlockSpec(memory_space=pl.ANY),
                  pl.BlockSpec(memory_space=pl.ANY)],
        out_specs=[pl.BlockSpec((1, tm, d), tok), sample(rows)],
        scratch_shapes=[pltpu.VMEM((n_slots, g * page_rows, width), F32),
                        pltpu.VMEM((n_slots, g * page_rows, width), F32),
                        pltpu.SemaphoreType.DMA((2, n_slots)),
                        pltpu.VMEM((rows, 1), F32), pltpu.VMEM((rows, 1), F32), pltpu.VMEM((rows, V_HEAD), F32)],
    )
    return pl.pallas_call(
        kern,
        grid_spec=grid_spec,
        out_shape=[jax.ShapeDtypeStruct((bsz, seq, d), F32),
                   jax.ShapeDtypeStruct((sample_bsz, rows, width), F32)],
        compiler_params=_params(2),
        name="finish_attn",
    )(page_table, x, yc, ya, mod, w_out, ln1, w_gate, w_up, w_down, ln2, wq, k_new, v_new, cache_k, cache_v)


def _sample_finish(x, yc, o1, o2, lam_params, g, mod_rows, w_out, ln1, w_gate, w_up, w_down, ln2, alpha, lam_init):
    n, d = x.shape
    kern = functools.partial(_sample_finish_kernel, alpha=alpha, lam_init=lam_init)
    args = (x, yc, o1, o2, lam_params, g, mod_rows, w_out, ln1, w_gate, w_up, w_down, ln2)
    return pl.pallas_call(
        kern,
        grid=(1,),
        in_specs=[_resident(a.shape) for a in args],
        out_specs=pl.BlockSpec((n, d), lambda i: (0, 0)),
        out_shape=jax.ShapeDtypeStruct((n, d), F32),
        compiler_params=_params(1),
        name="sample_finish",
    )(*args)


def _sample_in_kernel(x_ref, mod_ref, w_ref, cw_ref, st1_ref, st2_ref, tab_ref,
                      yconv_ref, q_ref, k_ref, v_ref, u_ref, *, t_len, cc):
    d = x_ref.shape[1]
    xm = (x_ref[...] * (1.0 + mod_ref[:, d:2 * d]) + mod_ref[:, 0:d]).astype(BF16)
    z = jnp.dot(xm, w_ref[...], preferred_element_type=F32)
    h, gb, gc = z[:, 0:cc], z[:, cc:2 * cc], z[:, 2 * cc:3 * cc]
    u = gc * h
    n = u.shape[0]
    t = lax.broadcasted_iota(jnp.int32, (n, cc), 0) % t_len
    u1 = jnp.where(t >= 1, pltpu.roll(u, 1, 0), st1_ref[...])
    u2 = jnp.where(t >= 2, pltpu.roll(u, 2, 0), st2_ref[...])
    cw = cw_ref[...]
    yconv_ref[...] = (gb * (cw[0:1] * u2 + cw[1:2] * u1 + cw[2:3] * u)).astype(BF16)
    u_ref[...] = u
    hw = N_HEADS * HEAD_W
    q_ref[...] = (_rope(z[:, 3 * cc:3 * cc + hw], tab_ref) * Q_SCALE).astype(BF16)
    k_ref[...] = _rope(z[:, 3 * cc + hw:3 * cc + 2 * hw], tab_ref)
    v_ref[...] = z[:, 3 * cc + 2 * hw:]


def _sample_in(x, mod_rows, w_in, conv_w, st1, st2, tab, t_len):
    n, _ = x.shape
    cc = conv_w.shape[1]
    hw = N_HEADS * HEAD_W
    kern = functools.partial(_sample_in_kernel, t_len=t_len, cc=cc)
    args = (x, mod_rows, w_in, conv_w, st1, st2, tab)
    out_shapes = [jax.ShapeDtypeStruct((n, cc), BF16), jax.ShapeDtypeStruct((n, hw), BF16),
                  jax.ShapeDtypeStruct((n, hw), F32), jax.ShapeDtypeStruct((n, N_HEADS * V_HEAD), F32),
                  jax.ShapeDtypeStruct((n, cc), F32)]
    return pl.pallas_call(
        kern,
        grid=(1,),
        in_specs=[_resident(a.shape) for a in args],
        out_specs=[pl.BlockSpec(o.shape, lambda i: (0, 0)) for o in out_shapes],
        out_shape=out_shapes,
        compiler_params=_params(1),
        name="sample_in",
    )(*args)


def _query_rows(q, bsz, t_len):
    q5 = q.reshape(bsz, t_len, N_HEADS, 2, QK_HEAD).transpose(0, 2, 3, 1, 4)
    eye_c = jnp.eye(2, dtype=q.dtype)
    w = q5[:, :, :, :, None, :] * eye_c[None, None, :, None, :, None]
    return w.reshape(bsz, N_HEADS * 2 * t_len, HEAD_W)


def _token_head_rows(a, bsz, t_len, rows):
    a = a.reshape(bsz, t_len * N_HEADS, a.shape[1] // N_HEADS)
    return jnp.pad(a, ((0, 0), (0, rows - t_len * N_HEADS), (0, 0)))


def _component_outputs(o, bsz, t_len):
    o5 = o.reshape(bsz, N_HEADS, 2, t_len, V_HEAD).transpose(2, 0, 3, 1, 4)
    o5 = o5.reshape(2, bsz * t_len, N_HEADS * V_HEAD)
    return o5[0], o5[1]


def kernel(x_prompt, x_sample, c_prompt, c_sample, cache_k, cache_v, state_conv, page_table, w_ada, b_ada, w_in, conv_w, lambda_q1, lambda_k1, lambda_q2, lambda_k2, subln_g, w_out, ln1_g, ln1_b, w_gate, w_up, w_down, ln2_g, ln2_b):
    bp, s_len, d = x_prompt.shape
    bs, t_len, _ = x_sample.shape
    depth, n_pool, page = cache_k.shape[:3]
    past = page_table.shape[1] * page
    cc = conv_w.shape[2]
    alpha = (2.0 * depth) ** 0.25
    assert (bs * t_len) % SUBLANES == 0 and t_len <= page

    ts = min(1024, s_len)
    tq = min(1024, s_len)
    tm = min(512, s_len)

    tab_p = jnp.asarray(_rope_tables(np.arange(s_len)))
    tab_s = jnp.asarray(np.tile(_rope_tables(past + np.arange(t_len)), (1, bs, 1)))
    ck = cache_k.reshape(depth * n_pool, page * N_HEADS, HEAD_W)
    cv = cache_v.reshape(depth * n_pool, page * N_HEADS, V_HEAD)
    c_all = jnp.concatenate([c_prompt, jnp.repeat(c_sample, t_len, axis=0)], axis=0)

    xp, xs = x_prompt, x_sample.reshape(bs * t_len, d)
    kp_l, vp_l, cp_l, ks_l, vs_l, cs_l = [], [], [], [], [], []
    for l in range(depth):
        lam_init = 0.8 - 0.6 * math.exp(-0.3 * l)
        w_in_b, w_out_b = w_in[l].astype(BF16), w_out[l].astype(BF16)
        w_gate_b, w_up_b, w_down_b = w_gate[l].astype(BF16), w_up[l].astype(BF16), w_down[l].astype(BF16)
        lam_params = jnp.stack([lambda_q1[l], lambda_k1[l], lambda_q2[l], lambda_k2[l]])
        g = subln_g[l].reshape(1, V_HEAD)
        ln1 = jnp.stack([ln1_g[l], ln1_b[l]])
        ln2 = jnp.stack([ln2_g[l], ln2_b[l]])

        mod = _adaln(c_all, w_ada[l], b_ada[l])
        mod_p = mod[:bp].reshape(bp, N_MOD, d)
        mod_s = mod[bp:]

        yconv, q, kb, vt, k, v, cst = _prompt_in(xp, mod_p, w_in_b, conv_w[l], tab_p, ts)
        yattn = _prompt_attn(q, kb, vt, lam_params, g, lam_init, tq)
        kp_l.append(k.reshape(bp, s_len, N_HEADS, HEAD_W))
        vp_l.append(v.reshape(bp, s_len, N_HEADS, V_HEAD))
        cp_l.append(cst[-1])

        st = state_conv[l]
        zero = jnp.zeros((bs, 1, cc), F32)
        st1 = jnp.concatenate([st[:, 1:2]] + [zero] * (t_len - 1), axis=1).reshape(bs * t_len, cc)
        st2 = jnp.concatenate([st[:, 0:1], st[:, 1:2]] + [zero] * (t_len - 2), axis=1).reshape(bs * t_len, cc)
        yconv_s, q_s, k_s, v_s, u_s = _sample_in(xs, mod_s, w_in_b, conv_w[l], st1, st2, tab_s, t_len)
        wq = _query_rows(q_s, bs, t_len)
        rows = pl.cdiv(t_len * N_HEADS, LANES) * LANES

        xp, o = _finish_attn(xp, yconv, yattn, mod_p, w_out_b, ln1, w_gate_b, w_up_b, w_down_b, ln2, alpha, tm,
                             page_table, wq, _token_head_rows(k_s, bs, t_len, rows),
                             _token_head_rows(v_s, bs, t_len, rows), ck, cv, l * n_pool, t_len)
        o1, o2 = _component_outputs(o, bs, t_len)
        xs = _sample_finish(xs, yconv_s, o1, o2, lam_params, g, mod_s, w_out_b, ln1, w_gate_b, w_up_b, w_down_b,
                            ln2, alpha, lam_init)
        ks_l.append(k_s.reshape(bs, t_len, N_HEADS, HEAD_W))
        vs_l.append(v_s.reshape(bs, t_len, N_HEADS, V_HEAD))
        cs_l.append(u_s.reshape(bs, t_len, cc)[:, t_len - (CONV_K - 1):])

    return (xp, xs.reshape(bs, t_len, d), jnp.stack(kp_l), jnp.stack(vp_l), jnp.stack(cp_l),
            jnp.stack(ks_l), jnp.stack(vs_l), jnp.stack(cs_l))
```
